```python
import jax, jax.numpy as jnp
from jax import lax
import numpy as np

D_MODEL = 1024
BATCH = 8
SEQ = 4096
DEPTH = 4

MIX_WIDTH = D_MODEL // 2
N_BRANCH = 3
A_WIDTH = MIX_WIDTH
A_BLOCKS = 8
A_BLOCK = A_WIDTH // A_BLOCKS
A_CONV = 4
A_C = 8.0
B_HEADS = 4
B_DK = MIX_WIDTH // B_HEADS
B_DV = MIX_WIDTH // B_HEADS
B_WIDTH = B_HEADS * B_DV
B_CONV = 4
B_CHUNK = 64
C_HEADS = 4
C_DV = MIX_WIDTH // C_HEADS
C_DK = C_DV // 2
C_WIDTH = C_HEADS * C_DV
C_RANK = 16
C_TAU = 16.0
C_CHUNK = 64
FFN_HIDDEN = -(-(8 * D_MODEL) // (3 * 256)) * 256
IN_SPLITS = (
    A_WIDTH, A_WIDTH,
    3 * B_HEADS * B_DK, B_WIDTH, B_HEADS, B_HEADS,
    C_HEADS * C_DK, C_HEADS * C_DK, C_WIDTH, C_RANK, C_WIDTH,
    N_BRANCH * D_MODEL,
)
IN_WIDTH = sum(IN_SPLITS)

kernel_name = "hybrid_rglru_gdn_gla_deepnorm"


def _layer_norm(x, g, b, eps=1e-5):
    xf = x.astype(jnp.float32)
    mu = jnp.mean(xf, -1, keepdims=True)
    var = jnp.mean(jnp.square(xf - mu), -1, keepdims=True)
    return ((xf - mu) * lax.rsqrt(var + eps) * g + b).astype(x.dtype)


def _rms_norm(x, w, eps=1e-6):
    xf = x.astype(jnp.float32)
    return xf * lax.rsqrt(jnp.mean(xf * xf, -1, keepdims=True) + eps) * w


def _l2norm(x, eps=1e-6):
    return x * lax.rsqrt(jnp.sum(x * x, -1, keepdims=True) + eps)


def _causal_dwconv(x, w):
    K = w.shape[0]
    S = x.shape[1]
    xp = jnp.pad(x, ((0, 0), (K - 1, 0), (0, 0)))
    return sum(xp[:, k:k + S] * w[k] for k in range(K))


def _rg_lru(x, w_r, b_r, w_i, b_i, lam):
    Bsz, S, W = x.shape
    xb = x.reshape(Bsz, S, A_BLOCKS, A_BLOCK)
    r = jax.nn.sigmoid(jnp.einsum("bshi,hij->bshj", xb, w_r).reshape(Bsz, S, W) + b_r)
    i = jax.nn.sigmoid(jnp.einsum("bshi,hij->bshj", xb, w_i).reshape(Bsz, S, W) + b_i)
    log_a = -A_C * r * jax.nn.softplus(-lam)
    a = jnp.exp(log_a)
    u = jnp.sqrt(-jnp.expm1(2.0 * log_a)) * (i * x)

    def combine(left, right):
        a1, b1 = left
        a2, b2 = right
        return a1 * a2, a2 * b1 + b2

    _, h = lax.associative_scan(combine, (a, u), axis=1)
    return h


def _gated_delta_rule(q, k, v, g, beta):
    Bsz, S, H, DK = q.shape
    DV = v.shape[-1]
    C = B_CHUNK
    N = S // C
    q = q.reshape(Bsz, N, C, H, DK) * (DK ** -0.5)
    k = k.reshape(Bsz, N, C, H, DK)
    v = v.reshape(Bsz, N, C, H, DV)
    beta = beta.reshape(Bsz, N, C, H)
    g = jnp.cumsum(g.reshape(Bsz, N, C, H), axis=2)
    causal = jnp.tril(jnp.ones((C, C), bool))[None, None, None]
    strict = jnp.tril(jnp.ones((C, C), bool), -1)[None, None, None]
    gh = g.transpose(0, 1, 3, 2)
    diff = gh[..., :, None] - gh[..., None, :]
    decay = jnp.where(causal, jnp.exp(jnp.where(causal, diff, 0.0)), 0.0)
    kk = jnp.einsum("bnihd,bnjhd->bnhij", k, k)
    A = jnp.where(strict, beta.transpose(0, 1, 3, 2)[..., :, None] * kk * decay, 0.0)
    eye = jnp.eye(C, dtype=A.dtype)
    rhs = jnp.concatenate([v * beta[..., None], k * (beta * jnp.exp(g))[..., None]], -1)
    rhs = rhs.transpose(0, 1, 3, 2, 4)
    sol = lax.linalg.triangular_solve(A + eye, rhs, left_side=True, lower=True,
                                      unit_diagonal=True)
    u = sol[..., :DV]
    w = sol[..., DV:]
    attn = jnp.einsum("bnihd,bnjhd->bnhij", q, k) * decay
    q_g = (q * jnp.exp(g)[..., None]).transpose(1, 0, 3, 2, 4)
    k_dec = (k * jnp.exp(g[:, :, -1:, :] - g)[..., None]).transpose(1, 0, 3, 2, 4)
    d_last = jnp.exp(g[:, :, -1, :]).transpose(1, 0, 2)
    xs = (u.transpose(1, 0, 2, 3, 4), w.transpose(1, 0, 2, 3, 4), q_g, k_dec,
          attn.transpose(1, 0, 2, 3, 4), d_last)

    def step(state, inp):
        u_n, w_n, qg_n, kd_n, at_n, dl_n = inp
        v_new = u_n - jnp.einsum("bhck,bhkv->bhcv", w_n, state)
        o = jnp.einsum("bhck,bhkv->bhcv", qg_n, state) + jnp.einsum("bhij,bhjv->bhiv", at_n, v_new)
        state = state * dl_n[..., None, None] + jnp.einsum("bhck,bhcv->bhkv", kd_n, v_new)
        return state, o

    s0 = jnp.zeros((Bsz, H, DK, DV), q.dtype)
    _, o = lax.scan(step, s0, xs)
    return o.transpose(1, 0, 3, 2, 4).reshape(Bsz, S, H, DV)


def _gla(q, k, v, log_a):
    Bsz, S, H, DK = q.shape
    DV = v.shape[-1]
    C = C_CHUNK
    N = S // C
    q = q.reshape(Bsz, N, C, H, DK) * (DK ** -0.5)
    k = k.reshape(Bsz, N, C, H, DK)
    v = v.reshape(Bsz, N, C, H, DV)
    b = jnp.cumsum(log_a.reshape(Bsz, N, C, H, DK), axis=2)
    b_last = b[:, :, -1:]
    q_in = q * jnp.exp(b)
    k_in = k * jnp.exp(-b)
    k_dec = k * jnp.exp(b_last - b)
    causal = jnp.tril(jnp.ones((C, C), bool))[None, None, None]
    attn = jnp.where(causal, jnp.einsum("bnihd,bnjhd->bnhij", q_in, k_in), 0.0)
    o_intra = jnp.einsum("bnhij,bnjhv->bnihv", attn, v)
    xs = (q_in.transpose(1, 0, 2, 3, 4), k_dec.transpose(1, 0, 2, 3, 4),
          v.transpose(1, 0, 2, 3, 4), jnp.exp(b_last[:, :, 0]).transpose(1, 0, 2, 3))

    def step(state, inp):
        qi_n, kd_n, v_n, dl_n = inp
        o = jnp.einsum("bchk,bhkv->bchv", qi_n, state)
        state = state * dl_n[..., None] + jnp.einsum("bchk,bchv->bhkv", kd_n, v_n)
        return state, o

    s0 = jnp.zeros((Bsz, H, DK, DV), q.dtype)
    _, o_inter = lax.scan(step, s0, xs)
    o = o_intra + o_inter.transpose(1, 0, 2, 3, 4)
    return o.reshape(Bsz, S, H, DV)


def _mixer(x, w_in, a_conv_w, a_conv_b, a_w_r, a_b_r, a_w_i, a_b_i, a_lambda,
           b_conv_w, b_a_log, b_dt_bias, b_norm_w, c_w_g2, c_b_g2, c_norm_w,
           gate_b, w_branch, w_out):
    f32 = jnp.float32
    Bsz, S, D = x.shape
    proj = jnp.einsum("bsd,de->bse", x, w_in).astype(f32)
    offs = np.cumsum(IN_SPLITS)[:-1].tolist()
    (pa_x, pa_g, pb_qkv, pb_z, pb_beta, pb_alpha,
     pc_q, pc_k, pc_v, pc_g, pc_r, p_merge) = jnp.split(proj, offs, axis=-1)

    xa = _causal_dwconv(pa_x, a_conv_w) + a_conv_b
    ya = _rg_lru(xa, a_w_r, a_b_r, a_w_i, a_b_i, a_lambda) * jax.nn.gelu(pa_g)

    qkv = jax.nn.silu(_causal_dwconv(pb_qkv, b_conv_w))
    bq, bk, bv = jnp.split(qkv, 3, axis=-1)
    bq = _l2norm(bq.reshape(Bsz, S, B_HEADS, B_DK))
    bk = _l2norm(bk.reshape(Bsz, S, B_HEADS, B_DK))
    bv = bv.reshape(Bsz, S, B_HEADS, B_DV)
    beta = jax.nn.sigmoid(pb_beta)
    g = -jnp.exp(b_a_log.astype(f32)) * jax.nn.softplus(pb_alpha + b_dt_bias)
    ob = _gated_delta_rule(bq, bk, bv, g, beta)
    yb = (_rms_norm(ob, b_norm_w) * jax.nn.silu(pb_z.reshape(Bsz, S, B_HEADS, B_DV))
          ).reshape(Bsz, S, B_WIDTH)

    log_a = jax.nn.log_sigmoid(jnp.einsum("bsr,rk->bsk", pc_g, c_w_g2) + c_b_g2) / C_TAU
    oc = _gla(pc_q.reshape(Bsz, S, C_HEADS, C_DK), pc_k.reshape(Bsz, S, C_HEADS, C_DK),
              pc_v.reshape(Bsz, S, C_HEADS, C_DV), log_a.reshape(Bsz, S, C_HEADS, C_DK))
    yc = (_rms_norm(oc, c_norm_w) * jax.nn.silu(pc_r.reshape(Bsz, S, C_HEADS, C_DV))
          ).reshape(Bsz, S, C_WIDTH)

    gates = jax.nn.sigmoid(p_merge + gate_b).reshape(Bsz, S, N_BRANCH, D)
    merged = sum(gates[:, :, gi] * jnp.einsum("bsc,cd->bsd", y, w_branch[gi])
                 for gi, y in enumerate((ya, yb, yc)))
    return jnp.einsum("bsd,de->bse", merged, w_out).astype(x.dtype)


def _swiglu(x, w1, w3, w2):
    h = jax.nn.silu(jnp.einsum("bsd,df->bsf", x, w1)) * jnp.einsum("bsd,df->bsf", x, w3)
    return jnp.einsum("bsf,fd->bsd", h, w2).astype(x.dtype)


def setup_inputs(seed: int = 0) -> dict:
    key = jax.random.key(seed)
    ks = iter(jax.random.split(key, 32))
    L, D = DEPTH, D_MODEL
    out_scale = (8.0 * DEPTH) ** -0.25

    def nrm(shape, scale):
        return jax.random.normal(next(ks), shape, jnp.float32) * scale

    x = nrm((BATCH, SEQ, D), 1.0)
    w_in = nrm((L, D, IN_WIDTH), D ** -0.5)
    a_conv_w = nrm((L, A_CONV, A_WIDTH), A_CONV ** -0.5)
    a_conv_b = nrm((L, A_WIDTH), 0.02)
    a_w_r = nrm((L, A_BLOCKS, A_BLOCK, A_BLOCK), A_BLOCK ** -0.5)
    a_b_r = nrm((L, A_WIDTH), 0.02)
    a_w_i = nrm((L, A_BLOCKS, A_BLOCK, A_BLOCK), A_BLOCK ** -0.5)
    a_b_i = nrm((L, A_WIDTH), 0.02)
    a_pow = jax.random.uniform(next(ks), (L, A_WIDTH), jnp.float32, 0.9, 0.999)
    a_root = a_pow ** (1.0 / A_C)
    a_lambda = jnp.log(a_root) - jnp.log1p(-a_root)
    b_conv_w = nrm((L, B_CONV, 3 * B_HEADS * B_DK), B_CONV ** -0.5)
    b_a_log = jnp.log(jax.random.uniform(next(ks), (L, B_HEADS), jnp.float32, 1.0, 16.0))
    dt = jnp.exp(jax.random.uniform(next(ks), (L, B_HEADS), jnp.float32,
                                    np.log(1e-3), np.log(1e-1)))
    b_dt_bias = dt + jnp.log(-jnp.expm1(-dt))
    b_norm_w = 1.0 + nrm((L, B_DV), 0.02)
    c_w_g2 = nrm((L, C_RANK, C_HEADS * C_DK), C_RANK ** -0.5)
    c_b_g2 = nrm((L, C_HEADS * C_DK), 0.02)
    c_norm_w = 1.0 + nrm((L, C_DV), 0.02)
    gate_b = nrm((L, N_BRANCH * D), 0.02)
    w_branch = nrm((L, N_BRANCH, MIX_WIDTH, D), MIX_WIDTH ** -0.5)
    w_out = nrm((L, D, D), D ** -0.5 * out_scale)
    ln1_g = 1.0 + nrm((L, D), 0.02)
    ln1_b = nrm((L, D), 0.02)
    ffn_w1 = nrm((L, D, FFN_HIDDEN), D ** -0.5)
    ffn_w3 = nrm((L, D, FFN_HIDDEN), D ** -0.5)
    ffn_w2 = nrm((L, FFN_HIDDEN, D), FFN_HIDDEN ** -0.5 * out_scale)
    ln2_g = 1.0 + nrm((L, D), 0.02)
    ln2_b = nrm((L, D), 0.02)
    return {"x": x, "w_in": w_in, "a_conv_w": a_conv_w, "a_conv_b": a_conv_b,
            "a_w_r": a_w_r, "a_b_r": a_b_r, "a_w_i": a_w_i, "a_b_i": a_b_i,
            "a_lambda": a_lambda, "b_conv_w": b_conv_w, "b_a_log": b_a_log,
            "b_dt_bias": b_dt_bias, "b_norm_w": b_norm_w, "c_w_g2": c_w_g2,
            "c_b_g2": c_b_g2, "c_norm_w": c_norm_w, "gate_b": gate_b,
            "w_branch": w_branch, "w_out": w_out, "ln1_g": ln1_g, "ln1_b": ln1_b,
            "ffn_w1": ffn_w1, "ffn_w3": ffn_w3, "ffn_w2": ffn_w2,
            "ln2_g": ln2_g, "ln2_b": ln2_b}


def reference(x, w_in, a_conv_w, a_conv_b, a_w_r, a_b_r, a_w_i, a_b_i, a_lambda,
              b_conv_w, b_a_log, b_dt_bias, b_norm_w, c_w_g2, c_b_g2, c_norm_w,
              gate_b, w_branch, w_out, ln1_g, ln1_b, ffn_w1, ffn_w3, ffn_w2,
              ln2_g, ln2_b):
    alpha = (2.0 * DEPTH) ** 0.25
    for l in range(DEPTH):
        mix = _mixer(x, w_in[l], a_conv_w[l], a_conv_b[l], a_w_r[l], a_b_r[l], a_w_i[l],
                     a_b_i[l], a_lambda[l], b_conv_w[l], b_a_log[l], b_dt_bias[l],
                     b_norm_w[l], c_w_g2[l], c_b_g2[l], c_norm_w[l], gate_b[l],
                     w_branch[l], w_out[l])
        x = _layer_norm(alpha * x + mix, ln1_g[l], ln1_b[l])
        ffn = _swiglu(x, ffn_w1[l], ffn_w3[l], ffn_w2[l])
        x = _layer_norm(alpha * x + ffn, ln2_g[l], ln2_b[l])
    return x
```

```python
import functools

import jax
import jax.numpy as jnp
import numpy as np
from jax import lax
from jax.experimental import pallas as pl
from jax.experimental.pallas import tpu as pltpu

F32 = jnp.float32
MXU_DTYPE = jnp.bfloat16

V7X_LANES = 128
V7X_SUBLANES = 8
V7X_VMEM_LIMIT_BYTES = 56 * 1024 * 1024

N_BRANCH = 3
A_BLOCKS = 8
A_C = 8.0
CONV_K = 4
B_HEADS = 4
C_HEADS = 4
C_TAU = 16.0
GLA_CHUNK = 64
INV_BASE = 8
LN_EPS = 1e-5
RMS_EPS = 1e-6
L2_EPS = 1e-6
SMALL_W = V7X_LANES
CONV_PAD = V7X_SUBLANES


def _mx(x):
    return x.astype(MXU_DTYPE)


def _dot(a, b):
    return jnp.dot(a, b, preferred_element_type=F32)


def _dot_nt(a, b):
    return lax.dot_general(a, b, (((1,), (1,)), ((), ())), preferred_element_type=F32)


def _dot_tn(a, b):
    return lax.dot_general(a, b, (((0,), (0,)), ((), ())), preferred_element_type=F32)


def _sigmoid(x):
    return 1.0 / (1.0 + jnp.exp(-x))


def _silu(x):
    return x * _sigmoid(x)


def _softplus(x):
    return jnp.maximum(x, 0.0) + jnp.log1p(jnp.exp(-jnp.abs(x)))


def _gelu_tanh(x):
    c = np.float32(np.sqrt(2.0 / np.pi))
    return 0.5 * x * (1.0 + jnp.tanh(c * (x + 0.044715 * (x * x * x))))


def _row_iota(shape):
    return lax.broadcasted_iota(jnp.int32, shape, 0)


def _col_iota(shape):
    return lax.broadcasted_iota(jnp.int32, shape, 1)


def _chunk_cumsum(x, chunk):
    pos = _row_iota(x.shape) % chunk
    d = 1
    while d < chunk:
        x = x + jnp.where(pos >= d, pltpu.roll(x, d, 0), 0.0)
        d *= 2
    return x


def _causal_conv(ext_ref, x_ref, w_ref, first_tile):
    ts = x_ref.shape[0]

    @pl.when(first_tile)
    def _():
        ext_ref[0:CONV_PAD, :] = jnp.zeros((CONV_PAD, ext_ref.shape[1]), F32)

    @pl.when(jnp.logical_not(first_tile))
    def _():
        ext_ref[0:CONV_PAD, :] = ext_ref[ts:ts + CONV_PAD, :]

    ext_ref[CONV_PAD:CONV_PAD + ts, :] = x_ref[...]
    acc = None
    for k in range(CONV_K):
        off = CONV_PAD - (CONV_K - 1) + k
        term = ext_ref[off:off + ts, :] * w_ref[k:k + 1, :]
        acc = term if acc is None else acc + term
    return acc


def _in_proj_kernel(x_ref, w_ref, ws_ref, o_ref, os_ref):
    xb = _mx(x_ref[...])
    o_ref[...] = _dot(xb, w_ref[...])

    @pl.when(pl.program_id(1) == 0)
    def _():
        os_ref[...] = _dot(xb, ws_ref[...])


def _in_proj(x, w_main, w_small, *, tm, tn):
    T, D = x.shape
    N = w_main.shape[1]
    return pl.pallas_call(
        _in_proj_kernel,
        out_shape=(jax.ShapeDtypeStruct((T, N), F32), jax.ShapeDtypeStruct((T, SMALL_W), F32)),
        grid=(T // tm, N // tn),
        in_specs=[
            pl.BlockSpec((tm, D), lambda i, j: (i, 0)),
            pl.BlockSpec((D, tn), lambda i, j: (0, j)),
            pl.BlockSpec((D, SMALL_W), lambda i, j: (0, 0)),
        ],
        out_specs=(
            pl.BlockSpec((tm, tn), lambda i, j: (i, j)),
            pl.BlockSpec((tm, SMALL_W), lambda i, j: (i, 0)),
        ),
        compiler_params=pltpu.CompilerParams(
            dimension_semantics=("parallel", "arbitrary"),
            vmem_limit_bytes=V7X_VMEM_LIMIT_BYTES),
        name="in_proj",
    )(x, w_main, w_small)


def _rglru_kernel(pax_ref, pag_ref, cw_ref, cb_ref, wr_ref, br_ref, wi_ref, bi_ref, lam_ref,
                  o_ref, ext_ref, a_ref, u_ref, h_ref, carry_ref):
    ts, W = pax_ref.shape
    first = pl.program_id(1) == 0
    xa = _causal_conv(ext_ref, pax_ref, cw_ref, first) + cb_ref[...]
    xab = _mx(xa)
    r = _sigmoid(_dot(xab, wr_ref[...]) + br_ref[...])
    i = _sigmoid(_dot(xab, wi_ref[...]) + bi_ref[...])
    log_a = (-A_C) * r * _softplus(-lam_ref[...])
    a = jnp.exp(log_a)
    a_ref[...] = a
    u_ref[...] = jnp.sqrt(1.0 - a * a) * (i * xa)

    @pl.when(first)
    def _():
        carry_ref[...] = jnp.zeros(carry_ref.shape, F32)

    rows = _row_iota((V7X_SUBLANES, W))

    def body(blk, h_prev):
        r0 = pl.multiple_of(blk * V7X_SUBLANES, V7X_SUBLANES)
        ca = a_ref[pl.ds(r0, V7X_SUBLANES), :]
        cb = u_ref[pl.ds(r0, V7X_SUBLANES), :]
        d = 1
        while d < V7X_SUBLANES:
            keep = rows >= d
            cb = jnp.where(keep, ca * pltpu.roll(cb, d, 0) + cb, cb)
            ca = jnp.where(keep, ca * pltpu.roll(ca, d, 0), ca)
            d *= 2
        h = ca * h_prev + cb
        h_ref[pl.ds(r0, V7X_SUBLANES), :] = h
        return jnp.broadcast_to(h[V7X_SUBLANES - 1:V7X_SUBLANES, :], (V7X_SUBLANES, W))

    h_last = lax.fori_loop(0, ts // V7X_SUBLANES, body, carry_ref[...], unroll=4)
    carry_ref[...] = h_last
    o_ref[...] = (h_ref[...] * _gelu_tanh(pag_ref[...])).astype(o_ref.dtype)


def _rglru(proj, cw, cb, wr, br, wi, bi, lam, *, B, S, ts, col_x, col_g):
    T = B * S
    W = cw.shape[1]
    nt = S // ts
    vec = lambda n: pl.BlockSpec((n, W), lambda b, t: (0, 0))
    mat = pl.BlockSpec((W, W), lambda b, t: (0, 0))
    return pl.pallas_call(
        _rglru_kernel,
        out_shape=jax.ShapeDtypeStruct((T, W), MXU_DTYPE),
        grid=(B, nt),
        in_specs=[
            pl.BlockSpec((ts, W), lambda b, t: (b * nt + t, col_x)),
            pl.BlockSpec((ts, W), lambda b, t: (b * nt + t, col_g)),
            vec(CONV_K), vec(1), mat, vec(1), mat, vec(1), vec(1),
        ],
        out_specs=pl.BlockSpec((ts, W), lambda b, t: (b * nt + t, 0)),
        scratch_shapes=[
            pltpu.VMEM((ts + CONV_PAD, W), F32),
            pltpu.VMEM((ts, W), F32),
            pltpu.VMEM((ts, W), F32),
            pltpu.VMEM((ts, W), F32),
            pltpu.VMEM((V7X_SUBLANES, W), F32),
        ],
        compiler_params=pltpu.CompilerParams(
            dimension_semantics=("parallel", "arbitrary"),
            vmem_limit_bytes=V7X_VMEM_LIMIT_BYTES),
        name="rglru",
    )(proj, proj, cw, cb, wr, br, wi, bi, lam)


def _gdn_kernel(qkv_ref, z_ref, sm_ref, cw_ref, alog_ref, dtb_ref, nw_ref,
                o_ref, ext_ref, state_ref):
    ts = qkv_ref.shape[0]
    dk = V7X_LANES
    hw = B_HEADS * dk
    first = pl.program_id(1) == 0

    @pl.when(first)
    def _():
        state_ref[...] = jnp.zeros(state_ref.shape, F32)

    qkv = _silu(_causal_conv(ext_ref, qkv_ref, cw_ref, first))
    sm = sm_ref[...]
    beta_all = _sigmoid(sm)
    g_all = -jnp.exp(alog_ref[...]) * _softplus(sm + dtb_ref[...])
    gcum_all = _chunk_cumsum(g_all, ts)
    gcum_t = gcum_all.T

    ri = _row_iota((ts, ts))
    ci = _col_iota((ts, ts))
    lower = ci <= ri
    strict = ci < ri
    eye = (ci == ri).astype(F32)
    scale = np.float32(dk ** -0.5)
    base_mask = (ri // INV_BASE) == (ci // INV_BASE)
    level_masks = []
    n = INV_BASE
    while n < ts:
        level_masks.append(jnp.logical_and((ri // (2 * n)) == (ci // (2 * n)),
                                           (ri // n) != (ci // n)))
        n *= 2

    for h in range(B_HEADS):
        q = qkv[:, h * dk:(h + 1) * dk]
        k = qkv[:, hw + h * dk:hw + (h + 1) * dk]
        v = qkv[:, 2 * hw + h * dk:2 * hw + (h + 1) * dk]
        q = q * lax.rsqrt(jnp.sum(q * q, -1, keepdims=True) + L2_EPS) * scale
        k = k * lax.rsqrt(jnp.sum(k * k, -1, keepdims=True) + L2_EPS)
        beta = beta_all[:, h:h + 1]
        gcol = gcum_all[:, B_HEADS + h:B_HEADS + h + 1]
        grow = gcum_t[B_HEADS + h:B_HEADS + h + 1, :]
        decay = jnp.where(lower, jnp.exp(jnp.where(lower, gcol - grow, 0.0)), 0.0)
        qb, kb = _mx(q), _mx(k)
        kk = _dot_nt(kb, kb)
        attn = _dot_nt(qb, kb) * decay
        a_mat = jnp.where(strict, beta * kk * decay, 0.0)
        a_base = jnp.where(base_mask, a_mat, 0.0)
        p_mat = eye - a_base
        x_pow = a_base
        span = 2
        while span < INV_BASE:
            xb = _mx(x_pow)
            x_pow = _dot(xb, xb)
            p_mat = p_mat + _dot(_mx(p_mat), _mx(x_pow))
            span *= 2
        for lm in level_masks:
            pb = _mx(p_mat)
            a21_t1 = _dot(_mx(jnp.where(lm, a_mat, 0.0)), pb)
            p_mat = p_mat - _dot(pb, _mx(a21_t1))
        e_g = jnp.exp(gcol)
        rhs = jnp.concatenate([v * beta, k * (beta * e_g)], axis=-1)
        sol = _dot(_mx(p_mat), _mx(rhs))
        u = sol[:, :dk]
        w = sol[:, dk:]
        g_last = gcol[ts - 1:ts, :]
        q_g = q * e_g
        k_dec = k * jnp.exp(g_last - gcol)
        state = state_ref[h]
        sb = _mx(state)
        v_new = u - _dot(_mx(w), sb)
        vb = _mx(v_new)
        o = _dot(_mx(q_g), sb) + _dot(_mx(attn), vb)
        state_ref[h] = state * jnp.exp(g_last) + _dot_tn(_mx(k_dec), vb)
        z = z_ref[:, h * dk:(h + 1) * dk]
        y = o * lax.rsqrt(jnp.mean(o * o, -1, keepdims=True) + RMS_EPS) * nw_ref[...] * _silu(z)
        o_ref[:, h * dk:(h + 1) * dk] = y.astype(o_ref.dtype)


def _gdn(proj, small, cw, alog, dtb, nw, *, B, S, ts, col_qkv, col_z):
    T = B * S
    nt = S // ts
    wq = cw.shape[1]
    wz = B_HEADS * V7X_LANES
    vec = lambda n, w: pl.BlockSpec((n, w), lambda b, t: (0, 0))
    return pl.pallas_call(
        _gdn_kernel,
        out_shape=jax.ShapeDtypeStruct((T, wz), MXU_DTYPE),
        grid=(B, nt),
        in_specs=[
            pl.BlockSpec((ts, wq), lambda b, t: (b * nt + t, col_qkv)),
            pl.BlockSpec((ts, wz), lambda b, t: (b * nt + t, col_z)),
            pl.BlockSpec((ts, SMALL_W), lambda b, t: (b * nt + t, 0)),
            vec(CONV_K, wq), vec(1, SMALL_W), vec(1, SMALL_W), vec(1, V7X_LANES),
        ],
        out_specs=pl.BlockSpec((ts, wz), lambda b, t: (b * nt + t, 0)),
        scratch_shapes=[
            pltpu.VMEM((ts + CONV_PAD, wq), F32),
            pltpu.VMEM((B_HEADS, V7X_LANES, V7X_LANES), F32),
        ],
        compiler_params=pltpu.CompilerParams(
            dimension_semantics=("parallel", "arbitrary"),
            vmem_limit_bytes=V7X_VMEM_LIMIT_BYTES),
        name="gdn",
    )(proj, proj, small, cw, alog, dtb, nw)


def _gla_kernel(qk_ref, v_ref, r_ref, sm_ref, wg_ref, bg_ref, nw_ref, o_ref, state_ref):
    ts = qk_ref.shape[0]
    C = GLA_CHUNK
    n_chunks = ts // C
    kw = qk_ref.shape[1] // 2
    dv = V7X_LANES
    pair_k = V7X_LANES
    dk = pair_k // 2
    n_pairs = kw // pair_k
    first = pl.program_id(1) == 0

    @pl.when(first)
    def _():
        state_ref[...] = jnp.zeros(state_ref.shape, F32)

    zg = _dot(_mx(sm_ref[...]), wg_ref[...]) + bg_ref[...]
    log_a = -_softplus(-zg) * np.float32(1.0 / C_TAU)
    b_cum = _chunk_cumsum(log_a, C)
    qk = qk_ref[...]
    q = qk[:, :kw] * np.float32(dk ** -0.5)
    k = qk[:, kw:]
    q_in = q * jnp.exp(b_cum)
    k_in = k * jnp.exp(-b_cum)

    ri = _row_iota((ts, ts))
    ci = _col_iota((ts, ts))
    causal = jnp.logical_and(ci <= ri, (ri // C) == (ci // C))
    lane_head = _col_iota((ts, pair_k)) // dk
    bd_mask = (_row_iota((pair_k, 2 * dv)) // dk) == (_col_iota((pair_k, 2 * dv)) // dv)

    for p in range(n_pairs):
        ks = slice(p * pair_k, (p + 1) * pair_k)
        q_p, kin_p, k_p, b_p = q_in[:, ks], k_in[:, ks], k[:, ks], b_cum[:, ks]
        v_p = v_ref[:, p * 2 * dv:(p + 1) * 2 * dv]
        vb_p = _mx(v_p)
        kinb = _mx(kin_p)
        o_intra = []
        for e in range(2):
            qm = _mx(jnp.where(lane_head == e, q_p, 0.0))
            attn = jnp.where(causal, _dot_nt(qm, kinb), 0.0)
            o_intra.append(_dot(_mx(attn), vb_p[:, e * dv:(e + 1) * dv]))
        o_intra = jnp.concatenate(o_intra, axis=-1)
        b_last_rows = [b_p[(c + 1) * C - 1:(c + 1) * C, :] for c in range(n_chunks)]
        b_last_blk = jnp.concatenate(
            b_last_rows + [jnp.zeros((V7X_LANES - n_chunks, pair_k), F32)], axis=0)
        d_last_cols = jnp.exp(b_last_blk).T
        state = state_ref[p]
        o_inter = []
        for c in range(n_chunks):
            rs = slice(c * C, (c + 1) * C)
            o_inter.append(_dot(_mx(q_p[rs]), _mx(state)))
            k_dec = k_p[rs] * jnp.exp(b_last_rows[c] - b_p[rs])
            upd = _dot_tn(_mx(k_dec), vb_p[rs])
            state = state * d_last_cols[:, c:c + 1] + jnp.where(bd_mask, upd, 0.0)
        state_ref[p] = state
        o_pair = o_intra + jnp.concatenate(o_inter, axis=0)
        for e in range(2):
            hcol = slice((2 * p + e) * dv, (2 * p + e + 1) * dv)
            o = o_pair[:, e * dv:(e + 1) * dv]
            y = (o * lax.rsqrt(jnp.mean(o * o, -1, keepdims=True) + RMS_EPS) * nw_ref[...]
                 * _silu(r_ref[:, hcol]))
            o_ref[:, hcol] = y.astype(o_ref.dtype)


def _gla(proj, small, wg, bg, nw, *, B, S, ts, col_qk, col_v, col_r):
    T = B * S
    nt = S // ts
    kw = wg.shape[1]
    wv = C_HEADS * V7X_LANES
    vec = lambda n, w: pl.BlockSpec((n, w), lambda b, t: (0, 0))
    return pl.pallas_call(
        _gla_kernel,
        out_shape=jax.ShapeDtypeStruct((T, wv), MXU_DTYPE),
        grid=(B, nt),
        in_specs=[
            pl.BlockSpec((ts, 2 * kw), lambda b, t: (b * nt + t, col_qk)),
            pl.BlockSpec((ts, wv), lambda b, t: (b * nt + t, col_v)),
            pl.BlockSpec((ts, wv), lambda b, t: (b * nt + t, col_r)),
            pl.BlockSpec((ts, SMALL_W), lambda b, t: (b * nt + t, 0)),
            vec(SMALL_W, kw), vec(1, kw), vec(1, V7X_LANES),
        ],
        out_specs=pl.BlockSpec((ts, wv), lambda b, t: (b * nt + t, 0)),
        scratch_shapes=[pltpu.VMEM((kw // V7X_LANES, V7X_LANES, 2 * V7X_LANES), F32)],
        compiler_params=pltpu.CompilerParams(
            dimension_semantics=("parallel", "arbitrary"),
            vmem_limit_bytes=V7X_VMEM_LIMIT_BYTES),
        name="gla",
    )(proj, proj, proj, small, wg, bg, nw)


def _layer_norm(h, g, b):
    mu = jnp.mean(h, -1, keepdims=True)
    hc = h - mu
    var = jnp.mean(hc * hc, -1, keepdims=True)
    return hc * lax.rsqrt(var + LN_EPS) * g + b


def _merge_kernel(alpha, ya_ref, yb_ref, yc_ref, pm_ref, x_ref, wb_ref, gb_ref, wo_ref,
                  g_ref, b_ref, o_ref):
    D = x_ref.shape[1]
    merged = None
    for gi, y_ref in enumerate((ya_ref, yb_ref, yc_ref)):
        gate = _sigmoid(pm_ref[:, gi * D:(gi + 1) * D] + gb_ref[:, gi * D:(gi + 1) * D])
        term = gate * _dot(y_ref[...], wb_ref[gi])
        merged = term if merged is None else merged + term
    mix = _dot(_mx(merged), wo_ref[...])
    o_ref[...] = _layer_norm(alpha * x_ref[...] + mix, g_ref[...], b_ref[...])


def _merge_out(ya, yb, yc, proj, x, wb, gb, wo, ln_g, ln_b, *, alpha, tm, col_merge):
    T, D = x.shape
    W = ya.shape[1]
    row = lambda w: pl.BlockSpec((tm, w), lambda i: (i, 0))
    return pl.pallas_call(
        functools.partial(_merge_kernel, alpha),
        out_shape=jax.ShapeDtypeStruct((T, D), F32),
        grid=(T // tm,),
        in_specs=[
            row(W), row(W), row(W),
            pl.BlockSpec((tm, N_BRANCH * D), lambda i: (i, col_merge)),
            row(D),
            pl.BlockSpec((N_BRANCH, W, D), lambda i: (0, 0, 0)),
            pl.BlockSpec((1, N_BRANCH * D), lambda i: (0, 0)),
            pl.BlockSpec((D, D), lambda i: (0, 0)),
            pl.BlockSpec((1, D), lambda i: (0, 0)),
            pl.BlockSpec((1, D), lambda i: (0, 0)),
        ],
        out_specs=row(D),
        compiler_params=pltpu.CompilerParams(
            dimension_semantics=("parallel",),
            vmem_limit_bytes=V7X_VMEM_LIMIT_BYTES),
        name="merge_out",
    )(ya, yb, yc, proj, x, wb, gb, wo, ln_g, ln_b)


def _ffn_kernel(alpha, x_ref, w1_ref, w3_ref, w2_ref, g_ref, b_ref, o_ref, xb_ref, acc_ref):
    j = pl.program_id(1)

    @pl.when(j == 0)
    def _():
        xb_ref[...] = _mx(x_ref[...])

    xb = xb_ref[...]
    h = _silu(_dot(xb, w1_ref[...])) * _dot(xb, w3_ref[...])
    part = _dot(_mx(h), w2_ref[...])

    @pl.when(j == 0)
    def _():
        acc_ref[...] = part

    @pl.when(j > 0)
    def _():
        acc_ref[...] += part

    @pl.when(j == pl.num_programs(1) - 1)
    def _():
        o_ref[...] = _layer_norm(alpha * x_ref[...] + acc_ref[...], g_ref[...], b_ref[...])


def _ffn(x, w1, w3, w2, ln_g, ln_b, *, alpha, tm, tf):
    T, D = x.shape
    Fh = w1.shape[1]
    return pl.pallas_call(
        functools.partial(_ffn_kernel, alpha),
        out_shape=jax.ShapeDtypeStruct((T, D), F32),
        grid=(T // tm, Fh // tf),
        in_specs=[
            pl.BlockSpec((tm, D), lambda i, j: (i, 0)),
            pl.BlockSpec((D, tf), lambda i, j: (0, j)),
            pl.BlockSpec((D, tf), lambda i, j: (0, j)),
            pl.BlockSpec((tf, D), lambda i, j: (j, 0)),
            pl.BlockSpec((1, D), lambda i, j: (0, 0)),
            pl.BlockSpec((1, D), lambda i, j: (0, 0)),
        ],
        out_specs=pl.BlockSpec((tm, D), lambda i, j: (i, 0)),
        scratch_shapes=[pltpu.VMEM((tm, D), MXU_DTYPE), pltpu.VMEM((tm, D), F32)],
        compiler_params=pltpu.CompilerParams(
            dimension_semantics=("parallel", "arbitrary"),
            vmem_limit_bytes=V7X_VMEM_LIMIT_BYTES),
        name="ffn",
    )(x, w1, w3, w2, ln_g, ln_b)


def _tile(n, pref):
    t = min(n, pref)
    while n % t:
        t -= 1
    return t


def _pad_lanes(v, offset, width):
    return jnp.zeros((1, width), F32).at[0, offset:offset + v.shape[0]].set(v)


def kernel(x, w_in, a_conv_w, a_conv_b, a_w_r, a_b_r, a_w_i, a_b_i, a_lambda, b_conv_w, b_a_log,
           b_dt_bias, b_norm_w, c_w_g2, c_b_g2, c_norm_w, gate_b, w_branch, w_out, ln1_g, ln1_b,
           ffn_w1, ffn_w3, ffn_w2, ln2_g, ln2_b):
    B, S, D = x.shape
    depth = w_in.shape[0]
    T = B * S
    a_width = a_conv_w.shape[2]
    qkv_w = b_conv_w.shape[2]
    b_width = B_HEADS * V7X_LANES
    c_kw = c_w_g2.shape[2]
    c_rank = c_w_g2.shape[1]
    c_width = C_HEADS * V7X_LANES
    alpha = float((2.0 * depth) ** 0.25)

    splits = (a_width, a_width, qkv_w, b_width, B_HEADS, B_HEADS, c_kw, c_kw, c_width, c_rank,
              c_width, N_BRANCH * D)
    offs = np.concatenate([[0], np.cumsum(splits)])
    (s_ax, s_ag, s_qkv, s_z, s_beta, s_alpha, s_cq, s_ck, s_cv, s_cg, s_cr, s_merge) = [
        slice(int(offs[i]), int(offs[i + 1])) for i in range(len(splits))]
    main_groups = (s_merge, s_qkv, s_ax, s_ag, s_z, s_cv, s_cr, s_cq, s_ck)
    w_main = jnp.concatenate([w_in[:, :, g] for g in main_groups], axis=-1).astype(MXU_DTYPE)
    n_small = 2 * B_HEADS + c_rank
    w_small = jnp.concatenate(
        [w_in[:, :, s_beta], w_in[:, :, s_alpha], w_in[:, :, s_cg],
         jnp.zeros((depth, D, SMALL_W - n_small), F32)], axis=-1).astype(MXU_DTYPE)
    col = {}
    pos = 0
    for name, g in zip(("merge", "qkv", "ax", "ag", "z", "cv", "cr", "cq", "ck"), main_groups):
        width = g.stop - g.start
        col[name] = (pos, width)
        pos += width
    n_main = pos

    def blk(name, width=None):
        start, w = col[name]
        w = width or w
        assert start % w == 0, (name, start, w)
        return start // w

    eye_blocks = jnp.eye(A_BLOCKS, dtype=F32)

    def block_diag(w):
        return jnp.einsum("lhij,hg->lhigj", w, eye_blocks).reshape(depth, a_width, a_width)

    wr_bd = block_diag(a_w_r).astype(MXU_DTYPE)
    wi_bd = block_diag(a_w_i).astype(MXU_DTYPE)
    wg_pad = jnp.zeros((depth, SMALL_W, c_kw), F32).at[:, 2 * B_HEADS:n_small, :].set(c_w_g2)
    wg_pad = wg_pad.astype(MXU_DTYPE)
    wb = w_branch.astype(MXU_DTYPE)
    wo = w_out.astype(MXU_DTYPE)
    w1 = ffn_w1.astype(MXU_DTYPE)
    w3 = ffn_w3.astype(MXU_DTYPE)
    w2 = ffn_w2.astype(MXU_DTYPE)

    tm_proj = _tile(T, 1024)
    tn_proj = _tile(n_main, 1536)
    ts_a = _tile(S, 512)
    ts_b = _tile(S, 256)
    ts_c = _tile(S, 256)
    tm_merge = _tile(T, 512)
    tm_ffn = _tile(T, 512)
    tf_ffn = ffn_w1.shape[2] // 2
    assert tf_ffn % V7X_LANES == 0
    assert ts_c % GLA_CHUNK == 0

    h = x.reshape(T, D)
    for l in range(depth):
        proj, small = _in_proj(h, w_main[l], w_small[l], tm=tm_proj, tn=tn_proj)
        ya = _rglru(proj, a_conv_w[l], a_conv_b[l][None], wr_bd[l], a_b_r[l][None], wi_bd[l],
                    a_b_i[l][None], a_lambda[l][None], B=B, S=S, ts=ts_a,
                    col_x=blk("ax"), col_g=blk("ag"))
        yb = _gdn(proj, small, b_conv_w[l], _pad_lanes(b_a_log[l], B_HEADS, SMALL_W),
                  _pad_lanes(b_dt_bias[l], B_HEADS, SMALL_W), b_norm_w[l][None],
                  B=B, S=S, ts=ts_b, col_qkv=blk("qkv"), col_z=blk("z"))
        yc = _gla(proj, small, wg_pad[l], c_b_g2[l][None], c_norm_w[l][None],
                  B=B, S=S, ts=ts_c, col_qk=blk("cq", 2 * c_kw), col_v=blk("cv"), col_r=blk("cr"))
        h = _merge_out(ya, yb, yc, proj, h, wb[l], gate_b[l][None], wo[l], ln1_g[l][None],
                       ln1_b[l][None], alpha=alpha, tm=tm_merge, col_merge=blk("merge"))
        h = _ffn(h, w1[l], w3[l], w2[l], ln2_g[l][None], ln2_b[l][None],
                 alpha=alpha, tm=tm_ffn, tf=tf_ffn)
    return h.reshape(B, S, D)
```

```python
import functools

import jax
import jax.numpy as jnp
import numpy as np
from jax import lax
from jax.experimental import pallas as pl
from jax.experimental.pallas import tpu as pltpu

F32 = jnp.float32
MXU_DTYPE = jnp.bfloat16

V7X_LANES = 128
V7X_SUBLANES = 8
V7X_VMEM_LIMIT_BYTES = 56 * 1024 * 1024

N_BRANCH = 3
A_BLOCKS = 8
A_C = 8.0
CONV_K = 4
B_HEADS = 4
C_HEADS = 4
C_TAU = 16.0
GLA_CHUNK = 64
GDN_CHUNK = 256
INV_BASE = 8
LN_EPS = 1e-5
RMS_EPS = 1e-6
L2_EPS = 1e-6
SMALL_W = V7X_LANES
CONV_PAD = V7X_SUBLANES


def _mx(x):
    return x.astype(MXU_DTYPE)


def _dot(a, b):
    return jnp.dot(a, b, preferred_element_type=F32)


def _dot_nt(a, b):
    return lax.dot_general(a, b, (((1,), (1,)), ((), ())), preferred_element_type=F32)


def _dot_tn(a, b):
    return lax.dot_general(a, b, (((0,), (0,)), ((), ())), preferred_element_type=F32)


def _sigmoid(x):
    return 1.0 / (1.0 + jnp.exp(-x))


def _silu(x):
    return x * _sigmoid(x)


def _softplus(x):
    return jnp.maximum(x, 0.0) + jnp.log1p(jnp.exp(-jnp.abs(x)))


def _gelu_tanh(x):
    c = np.float32(np.sqrt(2.0 / np.pi))
    return 0.5 * x * (1.0 + jnp.tanh(c * (x + 0.044715 * (x * x * x))))


def _row_iota(shape):
    return lax.broadcasted_iota(jnp.int32, shape, 0)


def _col_iota(shape):
    return lax.broadcasted_iota(jnp.int32, shape, 1)


def _chunk_cumsum(x, chunk):
    pos = _row_iota(x.shape) % chunk
    d = 1
    while d < chunk:
        x = x + jnp.where(pos >= d, pltpu.roll(x, d, 0), 0.0)
        d *= 2
    return x


def _causal_conv(ext_ref, x_ref, w_ref, first_tile):
    ts = x_ref.shape[0]

    @pl.when(first_tile)
    def _():
        ext_ref[0:CONV_PAD, :] = jnp.zeros((CONV_PAD, ext_ref.shape[1]), F32)

    @pl.when(jnp.logical_not(first_tile))
    def _():
        ext_ref[0:CONV_PAD, :] = ext_ref[ts:ts + CONV_PAD, :]

    ext_ref[CONV_PAD:CONV_PAD + ts, :] = x_ref[...]
    acc = None
    for k in range(CONV_K):
        off = CONV_PAD - (CONV_K - 1) + k
        term = ext_ref[off:off + ts, :] * w_ref[k:k + 1, :]
        acc = term if acc is None else acc + term
    return acc


def _in_proj_kernel(x_ref, w_ref, ws_ref, o_ref, os_ref):
    xb = _mx(x_ref[...])
    o_ref[...] = _dot(xb, w_ref[...])

    @pl.when(pl.program_id(1) == 0)
    def _():
        os_ref[...] = _dot(xb, ws_ref[...])


def _in_proj(x, w_main, w_small, *, tm, tn):
    T, D = x.shape
    N = w_main.shape[1]
    return pl.pallas_call(
        _in_proj_kernel,
        out_shape=(jax.ShapeDtypeStruct((T, N), F32), jax.ShapeDtypeStruct((T, SMALL_W), F32)),
        grid=(T // tm, N // tn),
        in_specs=[
            pl.BlockSpec((tm, D), lambda i, j: (i, 0)),
            pl.BlockSpec((D, tn), lambda i, j: (0, j)),
            pl.BlockSpec((D, SMALL_W), lambda i, j: (0, 0)),
        ],
        out_specs=(
            pl.BlockSpec((tm, tn), lambda i, j: (i, j)),
            pl.BlockSpec((tm, SMALL_W), lambda i, j: (i, 0)),
        ),
        compiler_params=pltpu.CompilerParams(
            dimension_semantics=("parallel", "arbitrary"),
            vmem_limit_bytes=V7X_VMEM_LIMIT_BYTES),
        name="in_proj",
    )(x, w_main, w_small)


def _rglru_kernel(pax_ref, pag_ref, cw_ref, cb_ref, wr_ref, br_ref, wi_ref, bi_ref, lam_ref,
                  o_ref, ext_ref, a_ref, u_ref, h_ref, carry_ref):
    ts, W = pax_ref.shape
    first = pl.program_id(1) == 0
    xa = _causal_conv(ext_ref, pax_ref, cw_ref, first) + cb_ref[...]
    xab = _mx(xa)
    r = _sigmoid(_dot(xab, wr_ref[...]) + br_ref[...])
    i = _sigmoid(_dot(xab, wi_ref[...]) + bi_ref[...])
    log_a = (-A_C) * r * _softplus(-lam_ref[...])
    a = jnp.exp(log_a)
    a_ref[...] = a
    u_ref[...] = jnp.sqrt(1.0 - a * a) * (i * xa)

    @pl.when(first)
    def _():
        carry_ref[...] = jnp.zeros(carry_ref.shape, F32)

    rows = _row_iota((V7X_SUBLANES, W))

    def body(blk, h_prev):
        r0 = pl.multiple_of(blk * V7X_SUBLANES, V7X_SUBLANES)
        ca = a_ref[pl.ds(r0, V7X_SUBLANES), :]
        cb = u_ref[pl.ds(r0, V7X_SUBLANES), :]
        d = 1
        while d < V7X_SUBLANES:
            keep = rows >= d
            cb = jnp.where(keep, ca * pltpu.roll(cb, d, 0) + cb, cb)
            ca = jnp.where(keep, ca * pltpu.roll(ca, d, 0), ca)
            d *= 2
        h = ca * h_prev + cb
        h_ref[pl.ds(r0, V7X_SUBLANES), :] = h
        return jnp.broadcast_to(h[V7X_SUBLANES - 1:V7X_SUBLANES, :], (V7X_SUBLANES, W))

    h_last = lax.fori_loop(0, ts // V7X_SUBLANES, body, carry_ref[...], unroll=4)
    carry_ref[...] = h_last
    o_ref[...] = (h_ref[...] * _gelu_tanh(pag_ref[...])).astype(o_ref.dtype)


def _rglru(proj, cw, cb, wr, br, wi, bi, lam, *, B, S, ts, col_x, col_g):
    T = B * S
    W = cw.shape[1]
    nt = S // ts
    vec = lambda n: pl.BlockSpec((n, W), lambda b, t: (0, 0))
    mat = pl.BlockSpec((W, W), lambda b, t: (0, 0))
    return pl.pallas_call(
        _rglru_kernel,
        out_shape=jax.ShapeDtypeStruct((T, W), MXU_DTYPE),
        grid=(B, nt),
        in_specs=[
            pl.BlockSpec((ts, W), lambda b, t: (b * nt + t, col_x)),
            pl.BlockSpec((ts, W), lambda b, t: (b * nt + t, col_g)),
            vec(CONV_K), vec(1), mat, vec(1), mat, vec(1), vec(1),
        ],
        out_specs=pl.BlockSpec((ts, W), lambda b, t: (b * nt + t, 0)),
        scratch_shapes=[
            pltpu.VMEM((ts + CONV_PAD, W), F32),
            pltpu.VMEM((ts, W), F32),
            pltpu.VMEM((ts, W), F32),
            pltpu.VMEM((ts, W), F32),
            pltpu.VMEM((V7X_SUBLANES, W), F32),
        ],
        compiler_params=pltpu.CompilerParams(
            dimension_semantics=("parallel", "arbitrary"),
            vmem_limit_bytes=V7X_VMEM_LIMIT_BYTES),
        name="rglru",
    )(proj, proj, cw, cb, wr, br, wi, bi, lam)


def _gdn_kernel(qkv_ref, z_ref, sm_ref, cw_ref, alog_ref, dtb_ref, nw_ref,
                o_ref, ext_ref, state_ref):
    ts = qkv_ref.shape[0]
    C = GDN_CHUNK
    n_chunks = ts // C
    dk = V7X_LANES
    hw = B_HEADS * dk
    first = pl.program_id(1) == 0

    @pl.when(first)
    def _():
        state_ref[...] = jnp.zeros(state_ref.shape, F32)

    qkv = _silu(_causal_conv(ext_ref, qkv_ref, cw_ref, first))
    sm = sm_ref[...]
    beta_all = _sigmoid(sm)
    g_all = -jnp.exp(alog_ref[...]) * _softplus(sm + dtb_ref[...])
    gcum_all = _chunk_cumsum(g_all, C)
    gcum_t = gcum_all.T

    ri = _row_iota((C, C))
    ci = _col_iota((C, C))
    lower = ci <= ri
    strict = ci < ri
    eye = (ci == ri).astype(F32)
    scale = np.float32(dk ** -0.5)
    base_mask = (ri // INV_BASE) == (ci // INV_BASE)
    level_masks = []
    n = INV_BASE
    while n < C:
        lm = jnp.logical_and((ri // (2 * n)) == (ci // (2 * n)), (ri // n) != (ci // n))
        level_masks.append(lm.astype(F32).astype(MXU_DTYPE))
        n *= 2

    chains = [(c, h) for c in range(n_chunks) for h in range(B_HEADS)]
    st = {}
    for c, h in chains:
        rs = slice(c * C, (c + 1) * C)
        q = qkv[rs, h * dk:(h + 1) * dk]
        k = qkv[rs, hw + h * dk:hw + (h + 1) * dk]
        v = qkv[rs, 2 * hw + h * dk:2 * hw + (h + 1) * dk]
        q = q * lax.rsqrt(jnp.sum(q * q, -1, keepdims=True) + L2_EPS) * scale
        k = k * lax.rsqrt(jnp.sum(k * k, -1, keepdims=True) + L2_EPS)
        beta = beta_all[rs, h:h + 1]
        gcol = gcum_all[rs, B_HEADS + h:B_HEADS + h + 1]
        grow = gcum_t[B_HEADS + h:B_HEADS + h + 1, rs]
        decay = jnp.where(lower, jnp.exp(jnp.where(lower, gcol - grow, 0.0)), 0.0)
        qb, kb = _mx(q), _mx(k)
        kk = _dot_nt(kb, kb)
        attn_b = _mx(_dot_nt(qb, kb) * decay)
        a_mat = jnp.where(strict, beta * kk * decay, 0.0)
        a_base = jnp.where(base_mask, a_mat, 0.0)
        st[c, h] = dict(q=q, k=k, v=v, beta=beta, gcol=gcol, attn_b=attn_b, a_b=_mx(a_mat),
                        p=eye - a_base, x=a_base)
    span = 2
    while span < INV_BASE:
        for key in chains:
            d = st[key]
            xb = _mx(d["x"])
            d["x"] = _dot(xb, xb)
        for key in chains:
            d = st[key]
            d["p"] = d["p"] + _dot(_mx(d["p"]), _mx(d["x"]))
        span *= 2
    for lm in level_masks:
        for key in chains:
            d = st[key]
            d["pb"] = _mx(d["p"])
            d["m"] = _dot(d["a_b"] * lm, d["pb"])
        for key in chains:
            d = st[key]
            d["p"] = d["p"] - _dot(d["pb"], _mx(d["m"]))
    for key in chains:
        d = st[key]
        e_g = jnp.exp(d["gcol"])
        rhs = jnp.concatenate([d["v"] * d["beta"], d["k"] * (d["beta"] * e_g)], axis=-1)
        sol = _dot(_mx(d["p"]), _mx(rhs))
        g_last = d["gcol"][C - 1:C, :]
        d.update(u=sol[:, :dk], wb=_mx(sol[:, dk:]), qgb=_mx(d["q"] * e_g),
                 kdb=_mx(d["k"] * jnp.exp(g_last - d["gcol"])), dl=jnp.exp(g_last))
    states = [state_ref[h] for h in range(B_HEADS)]
    for c in range(n_chunks):
        rs = slice(c * C, (c + 1) * C)
        for h in range(B_HEADS):
            d = st[c, h]
            sb = _mx(states[h])
            v_new = d["u"] - _dot(d["wb"], sb)
            vb = _mx(v_new)
            o = _dot(d["qgb"], sb) + _dot(d["attn_b"], vb)
            states[h] = states[h] * d["dl"] + _dot_tn(d["kdb"], vb)
            z = z_ref[rs, h * dk:(h + 1) * dk]
            y = (o * lax.rsqrt(jnp.mean(o * o, -1, keepdims=True) + RMS_EPS) * nw_ref[...]
                 * _silu(z))
            o_ref[rs, h * dk:(h + 1) * dk] = y.astype(o_ref.dtype)
    for h in range(B_HEADS):
        state_ref[h] = states[h]


def _gdn(proj, small, cw, alog, dtb, nw, *, B, S, ts, col_qkv, col_z):
    T = B * S
    nt = S // ts
    wq = cw.shape[1]
    wz = B_HEADS * V7X_LANES
    vec = lambda n, w: pl.BlockSpec((n, w), lambda b, t: (0, 0))
    return pl.pallas_call(
        _gdn_kernel,
        out_shape=jax.ShapeDtypeStruct((T, wz), MXU_DTYPE),
        grid=(B, nt),
        in_specs=[
            pl.BlockSpec((ts, wq), lambda b, t: (b * nt + t, col_qkv)),
            pl.BlockSpec((ts, wz), lambda b, t: (b * nt + t, col_z)),
            pl.BlockSpec((ts, SMALL_W), lambda b, t: (b * nt + t, 0)),
            vec(CONV_K, wq), vec(1, SMALL_W), vec(1, SMALL_W), vec(1, V7X_LANES),
        ],
        out_specs=pl.BlockSpec((ts, wz), lambda b, t: (b * nt + t, 0)),
        scratch_shapes=[
            pltpu.VMEM((ts + CONV_PAD, wq), F32),
            pltpu.VMEM((B_HEADS, V7X_LANES, V7X_LANES), F32),
        ],
        compiler_params=pltpu.CompilerParams(
            dimension_semantics=("parallel", "arbitrary"),
            vmem_limit_bytes=V7X_VMEM_LIMIT_BYTES),
        name="gdn",
    )(proj, proj, small, cw, alog, dtb, nw)


def _gla_kernel(qk_ref, v_ref, r_ref, sm_ref, wg_ref, bg_ref, nw_ref, o_ref, state_ref):
    ts = qk_ref.shape[0]
    C = GLA_CHUNK
    n_chunks = ts // C
    kw = qk_ref.shape[1] // 2
    dv = V7X_LANES
    pair_k = V7X_LANES
    dk = pair_k // 2
    n_pairs = kw // pair_k
    first = pl.program_id(1) == 0

    @pl.when(first)
    def _():
        state_ref[...] = jnp.zeros(state_ref.shape, F32)

    zg = _dot(_mx(sm_ref[...]), wg_ref[...]) + bg_ref[...]
    log_a = -_softplus(-zg) * np.float32(1.0 / C_TAU)
    b_cum = _chunk_cumsum(log_a, C)
    qk = qk_ref[...]
    q = qk[:, :kw] * np.float32(dk ** -0.5)
    k = qk[:, kw:]
    q_in = q * jnp.exp(b_cum)
    k_in = k * jnp.exp(-b_cum)

    ri = _row_iota((ts, ts))
    ci = _col_iota((ts, ts))
    causal = jnp.logical_and(ci <= ri, (ri // C) == (ci // C))
    lane_head = _col_iota((ts, pair_k)) // dk
    bd_mask = (_row_iota((pair_k, 2 * dv)) // dk) == (_col_iota((pair_k, 2 * dv)) // dv)

    for p in range(n_pairs):
        ks = slice(p * pair_k, (p + 1) * pair_k)
        q_p, kin_p, k_p, b_p = q_in[:, ks], k_in[:, ks], k[:, ks], b_cum[:, ks]
        v_p = v_ref[:, p * 2 * dv:(p + 1) * 2 * dv]
        vb_p = _mx(v_p)
        kinb = _mx(kin_p)
        o_intra = []
        for e in range(2):
            qm = _mx(jnp.where(lane_head == e, q_p, 0.0))
            attn = jnp.where(causal, _dot_nt(qm, kinb), 0.0)
            o_intra.append(_dot(_mx(attn), vb_p[:, e * dv:(e + 1) * dv]))
        o_intra = jnp.concatenate(o_intra, axis=-1)
        b_last_rows = [b_p[(c + 1) * C - 1:(c + 1) * C, :] for c in range(n_chunks)]
        b_last_blk = jnp.concatenate(
            b_last_rows + [jnp.zeros((V7X_LANES - n_chunks, pair_k), F32)], axis=0)
        d_last_cols = jnp.exp(b_last_blk).T
        state = state_ref[p]
        o_inter = []
        for c in range(n_chunks):
            rs = slice(c * C, (c + 1) * C)
            o_inter.append(_dot(_mx(q_p[rs]), _mx(state)))
            k_dec = k_p[rs] * jnp.exp(b_last_rows[c] - b_p[rs])
            upd = _dot_tn(_mx(k_dec), vb_p[rs])
            state = state * d_last_cols[:, c:c + 1] + jnp.where(bd_mask, upd, 0.0)
        state_ref[p] = state
        o_pair = o_intra + jnp.concatenate(o_inter, axis=0)
        for e in range(2):
            hcol = slice((2 * p + e) * dv, (2 * p + e + 1) * dv)
            o = o_pair[:, e * dv:(e + 1) * dv]
            y = (o * lax.rsqrt(jnp.mean(o * o, -1, keepdims=True) + RMS_EPS) * nw_ref[...]
                 * _silu(r_ref[:, hcol]))
            o_ref[:, hcol] = y.astype(o_ref.dtype)


def _gla(proj, small, wg, bg, nw, *, B, S, ts, col_qk, col_v, col_r):
    T = B * S
    nt = S // ts
    kw = wg.shape[1]
    wv = C_HEADS * V7X_LANES
    vec = lambda n, w: pl.BlockSpec((n, w), lambda b, t: (0, 0))
    return pl.pallas_call(
        _gla_kernel,
        out_shape=jax.ShapeDtypeStruct((T, wv), MXU_DTYPE),
        grid=(B, nt),
        in_specs=[
            pl.BlockSpec((ts, 2 * kw), lambda b, t: (b * nt + t, col_qk)),
            pl.BlockSpec((ts, wv), lambda b, t: (b * nt + t, col_v)),
            pl.BlockSpec((ts, wv), lambda b, t: (b * nt + t, col_r)),
            pl.BlockSpec((ts, SMALL_W), lambda b, t: (b * nt + t, 0)),
            vec(SMALL_W, kw), vec(1, kw), vec(1, V7X_LANES),
        ],
        out_specs=pl.BlockSpec((ts, wv), lambda b, t: (b * nt + t, 0)),
        scratch_shapes=[pltpu.VMEM((kw // V7X_LANES, V7X_LANES, 2 * V7X_LANES), F32)],
        compiler_params=pltpu.CompilerParams(
            dimension_semantics=("parallel", "arbitrary"),
            vmem_limit_bytes=V7X_VMEM_LIMIT_BYTES),
        name="gla",
    )(proj, proj, proj, small, wg, bg, nw)


def _layer_norm(h, g, b):
    mu = jnp.mean(h, -1, keepdims=True)
    hc = h - mu
    var = jnp.mean(hc * hc, -1, keepdims=True)
    return hc * lax.rsqrt(var + LN_EPS) * g + b


def _merge_kernel(alpha, ya_ref, yb_ref, yc_ref, pm_ref, x_ref, wb_ref, gb_ref, wo_ref,
                  g_ref, b_ref, o_ref):
    D = x_ref.shape[1]
    merged = None
    for gi, y_ref in enumerate((ya_ref, yb_ref, yc_ref)):
        gate = _sigmoid(pm_ref[:, gi * D:(gi + 1) * D] + gb_ref[:, gi * D:(gi + 1) * D])
        term = gate * _dot(y_ref[...], wb_ref[gi])
        merged = term if merged is None else merged + term
    mix = _dot(_mx(merged), wo_ref[...])
    o_ref[...] = _layer_norm(alpha * x_ref[...] + mix, g_ref[...], b_ref[...])


def _merge_out(ya, yb, yc, proj, x, wb, gb, wo, ln_g, ln_b, *, alpha, tm, col_merge):
    T, D = x.shape
    W = ya.shape[1]
    row = lambda w: pl.BlockSpec((tm, w), lambda i: (i, 0))
    return pl.pallas_call(
        functools.partial(_merge_kernel, alpha),
        out_shape=jax.ShapeDtypeStruct((T, D), F32),
        grid=(T // tm,),
        in_specs=[
            row(W), row(W), row(W),
            pl.BlockSpec((tm, N_BRANCH * D), lambda i: (i, col_merge)),
            row(D),
            pl.BlockSpec((N_BRANCH, W, D), lambda i: (0, 0, 0)),
            pl.BlockSpec((1, N_BRANCH * D), lambda i: (0, 0)),
            pl.BlockSpec((D, D), lambda i: (0, 0)),
            pl.BlockSpec((1, D), lambda i: (0, 0)),
            pl.BlockSpec((1, D), lambda i: (0, 0)),
        ],
        out_specs=row(D),
        compiler_params=pltpu.CompilerParams(
            dimension_semantics=("parallel",),
            vmem_limit_bytes=V7X_VMEM_LIMIT_BYTES),
        name="merge_out",
    )(ya, yb, yc, proj, x, wb, gb, wo, ln_g, ln_b)


def _ffn_kernel(alpha, x_ref, w1_ref, w3_ref, w2_ref, g_ref, b_ref, o_ref, xb_ref, acc_ref):
    j = pl.program_id(1)

    @pl.when(j == 0)
    def _():
        xb_ref[...] = _mx(x_ref[...])

    xb = xb_ref[...]
    h = _silu(_dot(xb, w1_ref[...])) * _dot(xb, w3_ref[...])
    part = _dot(_mx(h), w2_ref[...])

    @pl.when(j == 0)
    def _():
        acc_ref[...] = part

    @pl.when(j > 0)
    def _():
        acc_ref[...] += part

    @pl.when(j == pl.num_programs(1) - 1)
    def _():
        o_ref[...] = _layer_norm(alpha * x_ref[...] + acc_ref[...], g_ref[...], b_ref[...])


def _ffn(x, w1, w3, w2, ln_g, ln_b, *, alpha, tm, tf):
    T, D = x.shape
    Fh = w1.shape[1]
    return pl.pallas_call(
        functools.partial(_ffn_kernel, alpha),
        out_shape=jax.ShapeDtypeStruct((T, D), F32),
        grid=(T // tm, Fh // tf),
        in_specs=[
            pl.BlockSpec((tm, D), lambda i, j: (i, 0)),
            pl.BlockSpec((D, tf), lambda i, j: (0, j)),
            pl.BlockSpec((D, tf), lambda i, j: (0, j)),
            pl.BlockSpec((tf, D), lambda i, j: (j, 0)),
            pl.BlockSpec((1, D), lambda i, j: (0, 0)),
            pl.BlockSpec((1, D), lambda i, j: (0, 0)),
        ],
        out_specs=pl.BlockSpec((tm, D), lambda i, j: (i, 0)),
        scratch_shapes=[pltpu.VMEM((tm, D), MXU_DTYPE), pltpu.VMEM((tm, D), F32)],
        compiler_params=pltpu.CompilerParams(
            dimension_semantics=("parallel", "arbitrary"),
            vmem_limit_bytes=V7X_VMEM_LIMIT_BYTES),
        name="ffn",
    )(x, w1, w3, w2, ln_g, ln_b)


def _tile(n, pref):
    t = min(n, pref)
    while n % t:
        t -= 1
    return t


def _pad_lanes(v, offset, width):
    return jnp.zeros((1, width), F32).at[0, offset:offset + v.shape[0]].set(v)


def kernel(x, w_in, a_conv_w, a_conv_b, a_w_r, a_b_r, a_w_i, a_b_i, a_lambda, b_conv_w, b_a_log,
           b_dt_bias, b_norm_w, c_w_g2, c_b_g2, c_norm_w, gate_b, w_branch, w_out, ln1_g, ln1_b,
           ffn_w1, ffn_w3, ffn_w2, ln2_g, ln2_b):
    B, S, D = x.shape
    depth = w_in.shape[0]
    T = B * S
    a_width = a_conv_w.shape[2]
    qkv_w = b_conv_w.shape[2]
    b_width = B_HEADS * V7X_LANES
    c_kw = c_w_g2.shape[2]
    c_rank = c_w_g2.shape[1]
    c_width = C_HEADS * V7X_LANES
    alpha = float((2.0 * depth) ** 0.25)

    splits = (a_width, a_width, qkv_w, b_width, B_HEADS, B_HEADS, c_kw, c_kw, c_width, c_rank,
              c_width, N_BRANCH * D)
    offs = np.concatenate([[0], np.cumsum(splits)])
    (s_ax, s_ag, s_qkv, s_z, s_beta, s_alpha, s_cq, s_ck, s_cv, s_cg, s_cr, s_merge) = [
        slice(int(offs[i]), int(offs[i + 1])) for i in range(len(splits))]
    main_groups = (s_merge, s_qkv, s_ax, s_ag, s_z, s_cv, s_cr, s_cq, s_ck)
    w_main = jnp.concatenate([w_in[:, :, g] for g in main_groups], axis=-1).astype(MXU_DTYPE)
    n_small = 2 * B_HEADS + c_rank
    w_small = jnp.concatenate(
        [w_in[:, :, s_beta], w_in[:, :, s_alpha], w_in[:, :, s_cg],
         jnp.zeros((depth, D, SMALL_W - n_small), F32)], axis=-1).astype(MXU_DTYPE)
    col = {}
    pos = 0
    for name, g in zip(("merge", "qkv", "ax", "ag", "z", "cv", "cr", "cq", "ck"), main_groups):
        width = g.stop - g.start
        col[name] = (pos, width)
        pos += width
    n_main = pos

    def blk(name, width=None):
        start, w = col[name]
        w = width or w
        assert start % w == 0, (name, start, w)
        return start // w

    eye_blocks = jnp.eye(A_BLOCKS, dtype=F32)

    def block_diag(w):
        return jnp.einsum("lhij,hg->lhigj", w, eye_blocks).reshape(depth, a_width, a_width)

    wr_bd = block_diag(a_w_r).astype(MXU_DTYPE)
    wi_bd = block_diag(a_w_i).astype(MXU_DTYPE)
    wg_pad = jnp.zeros((depth, SMALL_W, c_kw), F32).at[:, 2 * B_HEADS:n_small, :].set(c_w_g2)
    wg_pad = wg_pad.astype(MXU_DTYPE)
    wb = w_branch.astype(MXU_DTYPE)
    wo = w_out.astype(MXU_DTYPE)
    w1 = ffn_w1.astype(MXU_DTYPE)
    w3 = ffn_w3.astype(MXU_DTYPE)
    w2 = ffn_w2.astype(MXU_DTYPE)

    tm_proj = _tile(T, 1024)
    tn_proj = _tile(n_main, 1536)
    ts_a = _tile(S, 512)
    ts_b = _tile(S, 512)
    assert ts_b % GDN_CHUNK == 0
    ts_c = _tile(S, 256)
    tm_merge = _tile(T, 512)
    tm_ffn = _tile(T, 512)
    tf_ffn = ffn_w1.shape[2] // 2
    assert tf_ffn % V7X_LANES == 0
    assert ts_c % GLA_CHUNK == 0

    h = x.reshape(T, D)
    for l in range(depth):
        proj, small = _in_proj(h, w_main[l], w_small[l], tm=tm_proj, tn=tn_proj)
        ya = _rglru(proj, a_conv_w[l], a_conv_b[l][None], wr_bd[l], a_b_r[l][None], wi_bd[l],
                    a_b_i[l][None], a_lambda[l][None], B=B, S=S, ts=ts_a,
                    col_x=blk("ax"), col_g=blk("ag"))
        yb = _gdn(proj, small, b_conv_w[l], _pad_lanes(b_a_log[l], B_HEADS, SMALL_W),
                  _pad_lanes(b_dt_bias[l], B_HEADS, SMALL_W), b_norm_w[l][None],
                  B=B, S=S, ts=ts_b, col_qkv=blk("qkv"), col_z=blk("z"))
        yc = _gla(proj, small, wg_pad[l], c_b_g2[l][None], c_norm_w[l][None],
                  B=B, S=S, ts=ts_c, col_qk=blk("cq", 2 * c_kw), col_v=blk("cv"), col_r=blk("cr"))
        h = _merge_out(ya, yb, yc, proj, h, wb[l], gate_b[l][None], wo[l], ln1_g[l][None],
                       ln1_b[l][None], alpha=alpha, tm=tm_merge, col_merge=blk("merge"))
        h = _ffn(h, w1[l], w3[l], w2[l], ln2_g[l][None], ln2_b[l][None],
                 alpha=alpha, tm=tm_ffn, tf=tf_ffn)
    return h.reshape(B, S, D)
```

```python
import functools

import jax
import jax.numpy as jnp
import numpy as np
from jax import lax
from jax.experimental import pallas as pl
from jax.experimental.pallas import tpu as pltpu

F32 = jnp.float32
MXU_DTYPE = jnp.bfloat16

V7X_LANES = 128
V7X_SUBLANES = 8
V7X_VMEM_LIMIT_BYTES = 56 * 1024 * 1024

N_BRANCH = 3
A_BLOCKS = 8
A_C = 8.0
CONV_K = 4
B_HEADS = 4
C_HEADS = 4
C_TAU = 16.0
GLA_CHUNK = 64
GLA_ATTN_TILE = 256
GDN_CHUNK = 256
INV_BASE = 8
LN_EPS = 1e-5
RMS_EPS = 1e-6
L2_EPS = 1e-6
SMALL_W = V7X_LANES
CONV_PAD = V7X_SUBLANES


def _mx(x):
    return x.astype(MXU_DTYPE)


def _dot(a, b):
    return jnp.dot(a, b, preferred_element_type=F32)


def _dot_nt(a, b):
    return lax.dot_general(a, b, (((1,), (1,)), ((), ())), preferred_element_type=F32)


def _dot_tn(a, b):
    return lax.dot_general(a, b, (((0,), (0,)), ((), ())), preferred_element_type=F32)


def _sigmoid(x):
    return 1.0 / (1.0 + jnp.exp(-x))


def _silu(x):
    return x * _sigmoid(x)


def _softplus(x):
    return jnp.maximum(x, 0.0) + jnp.log1p(jnp.exp(-jnp.abs(x)))


def _gelu_tanh(x):
    c = np.float32(np.sqrt(2.0 / np.pi))
    return 0.5 * x * (1.0 + jnp.tanh(c * (x + 0.044715 * (x * x * x))))


def _row_iota(shape):
    return lax.broadcasted_iota(jnp.int32, shape, 0)


def _col_iota(shape):
    return lax.broadcasted_iota(jnp.int32, shape, 1)


def _chunk_cumsum(x, chunk):
    pos = _row_iota(x.shape) % chunk
    d = 1
    while d < chunk:
        x = x + jnp.where(pos >= d, pltpu.roll(x, d, 0), 0.0)
        d *= 2
    return x


def _causal_conv(hist_ref, x, w_ref, first_tile):
    ts = x.shape[0]
    hist = jnp.where(first_tile, 0.0, hist_ref[...])
    ext = jnp.concatenate([hist, x], axis=0)
    acc = x * w_ref[CONV_K - 1:CONV_K, :]
    for back in range(1, CONV_K):
        shifted = pltpu.roll(ext, back, 0)[CONV_PAD:CONV_PAD + ts, :]
        acc = acc + shifted * w_ref[CONV_K - 1 - back:CONV_K - back, :]
    hist_ref[...] = x[ts - CONV_PAD:ts, :]
    return acc


def _lookahead_tile(nt):
    s = pl.program_id(0)
    tile = jnp.maximum(s - 1, 0)
    return lax.rem(tile, nt) == 0


def _zero_on_first_step(*refs):
    @pl.when(pl.program_id(0) == 0)
    def _():
        for ref in refs:
            ref[...] = jnp.zeros(ref.shape, ref.dtype)


def _advance_projection(xb_ref, w_ref, pn_ref, cur_ref):
    cur_ref[...] = pn_ref[...]
    pn_ref[...] = _dot(xb_ref[...], w_ref[...])


def _mixer_call(body, xb, w, params, param_specs, *, n_tiles, ts, out_w, scratch, name):
    D = xb.shape[1]
    T = xb.shape[0]
    return pl.pallas_call(
        body,
        out_shape=jax.ShapeDtypeStruct((T, out_w), MXU_DTYPE),
        grid=(n_tiles + 1,),
        in_specs=[
            pl.BlockSpec((ts, D), lambda s: (jnp.minimum(s, n_tiles - 1), 0)),
            pl.BlockSpec(w.shape, lambda s: (0, 0)),
        ] + param_specs,
        out_specs=pl.BlockSpec((ts, out_w), lambda s: (jnp.maximum(s - 1, 0), 0)),
        scratch_shapes=[pltpu.VMEM((ts, w.shape[1]), F32),
                        pltpu.VMEM((ts, w.shape[1]), F32)
                        ] + scratch,
        compiler_params=pltpu.CompilerParams(
            dimension_semantics=("arbitrary",),
            vmem_limit_bytes=V7X_VMEM_LIMIT_BYTES),
        name=name,
    )(xb, w, *params)


def _const_spec(shape):
    return pl.BlockSpec(shape, lambda s: (0,) * len(shape))


def _rglru_kernel(nt, xb_ref, w_ref, cw_ref, cb_ref, wr_ref, br_ref, wi_ref, bi_ref, lam_ref,
                  o_ref, pn_ref, cur_ref, hist_ref, a_ref, u_ref, h_ref, carry_ref):
    ts = xb_ref.shape[0]
    W = cw_ref.shape[1]
    first = _lookahead_tile(nt)
    _zero_on_first_step(pn_ref, hist_ref, carry_ref)
    _advance_projection(xb_ref, w_ref, pn_ref, cur_ref)
    xa = _causal_conv(hist_ref, cur_ref[:, 0:W], cw_ref, first) + cb_ref[...]
    xab = _mx(xa)
    r = _sigmoid(_dot(xab, wr_ref[...]) + br_ref[...])
    i = _sigmoid(_dot(xab, wi_ref[...]) + bi_ref[...])
    log_a = (-A_C) * r * _softplus(-lam_ref[...])
    a = jnp.exp(log_a)
    a_ref[...] = a
    u_ref[...] = jnp.sqrt(1.0 - a * a) * (i * xa)
    rows = _row_iota((V7X_SUBLANES, W))
    h0 = jnp.where(first, 0.0, carry_ref[...])

    def body(blk, h_prev):
        r0 = pl.multiple_of(blk * V7X_SUBLANES, V7X_SUBLANES)
        ca = a_ref[pl.ds(r0, V7X_SUBLANES), :]
        cb = u_ref[pl.ds(r0, V7X_SUBLANES), :]
        d = 1
        while d < V7X_SUBLANES:
            keep = rows >= d
            cb = jnp.where(keep, ca * pltpu.roll(cb, d, 0) + cb, cb)
            ca = jnp.where(keep, ca * pltpu.roll(ca, d, 0), ca)
            d *= 2
        h = ca * h_prev + cb
        h_ref[pl.ds(r0, V7X_SUBLANES), :] = h
        return jnp.broadcast_to(h[V7X_SUBLANES - 1:V7X_SUBLANES, :], (V7X_SUBLANES, W))

    carry_ref[...] = lax.fori_loop(0, ts // V7X_SUBLANES, body, h0, unroll=4)
    o_ref[...] = (h_ref[...] * _gelu_tanh(cur_ref[:, W:2 * W])).astype(o_ref.dtype)


def _rglru(xb, w, cw, cb, wr, br, wi, bi, lam, *, n_tiles, nt, ts):
    W = cw.shape[1]
    params = (cw, cb, wr, br, wi, bi, lam)
    specs = [_const_spec(p.shape) for p in params]
    scratch = [
        pltpu.VMEM((CONV_PAD, W), F32),
        pltpu.VMEM((ts, W), F32),
        pltpu.VMEM((ts, W), F32),
        pltpu.VMEM((ts, W), F32),
        pltpu.VMEM((V7X_SUBLANES, W), F32),
    ]
    return _mixer_call(functools.partial(_rglru_kernel, nt), xb, w, params, specs,
                       n_tiles=n_tiles, ts=ts, out_w=W, scratch=scratch, name="rglru")


def _gdn_kernel(nt, xb_ref, w_ref, cw_ref, alog_ref, dtb_ref, nw_ref,
                o_ref, pn_ref, cur_ref, hist_ref, state_ref):
    ts = xb_ref.shape[0]
    C = GDN_CHUNK
    n_chunks = ts // C
    dk = V7X_LANES
    hw = B_HEADS * dk
    wq = cw_ref.shape[1]
    first = _lookahead_tile(nt)
    _zero_on_first_step(pn_ref, hist_ref, state_ref)
    _advance_projection(xb_ref, w_ref, pn_ref, cur_ref)
    sm = cur_ref[:, wq + hw:wq + hw + SMALL_W]
    qkv = _silu(_causal_conv(hist_ref, cur_ref[:, 0:wq], cw_ref, first))
    beta_all = _sigmoid(sm)
    g_all = -jnp.exp(alog_ref[...]) * _softplus(sm + dtb_ref[...])
    gcum_all = _chunk_cumsum(g_all, C)
    gcum_t = gcum_all.T

    ri = _row_iota((C, C))
    ci = _col_iota((C, C))
    lower = ci <= ri
    strict = ci < ri
    eye = (ci == ri).astype(F32)
    scale = np.float32(dk ** -0.5)
    base_mask = (ri // INV_BASE) == (ci // INV_BASE)
    level_masks = []
    n = INV_BASE
    while n < C:
        lm = jnp.logical_and((ri // (2 * n)) == (ci // (2 * n)), (ri // n) != (ci // n))
        level_masks.append(lm.astype(F32).astype(MXU_DTYPE))
        n *= 2

    chains = [(c, h) for c in range(n_chunks) for h in range(B_HEADS)]
    st = {}
    for c, h in chains:
        rs = slice(c * C, (c + 1) * C)
        q = qkv[rs, h * dk:(h + 1) * dk]
        k = qkv[rs, hw + h * dk:hw + (h + 1) * dk]
        v = qkv[rs, 2 * hw + h * dk:2 * hw + (h + 1) * dk]
        q = q * lax.rsqrt(jnp.sum(q * q, -1, keepdims=True) + L2_EPS) * scale
        k = k * lax.rsqrt(jnp.sum(k * k, -1, keepdims=True) + L2_EPS)
        beta = beta_all[rs, h:h + 1]
        gcol = gcum_all[rs, B_HEADS + h:B_HEADS + h + 1]
        grow = gcum_t[B_HEADS + h:B_HEADS + h + 1, rs]
        decay = jnp.where(lower, jnp.exp(jnp.where(lower, gcol - grow, 0.0)), 0.0)
        qb, kb = _mx(q), _mx(k)
        kk = _dot_nt(kb, kb)
        attn_b = _mx(_dot_nt(qb, kb) * decay)
        a_mat = jnp.where(strict, beta * kk * decay, 0.0)
        a_base = jnp.where(base_mask, a_mat, 0.0)
        st[c, h] = dict(q=q, k=k, v=v, beta=beta, gcol=gcol, attn_b=attn_b, a_b=_mx(a_mat),
                        p=eye - a_base, x=a_base)
    span = 2
    while span < INV_BASE:
        for key in chains:
            d = st[key]
            xb = _mx(d["x"])
            d["x"] = _dot(xb, xb)
        for key in chains:
            d = st[key]
            d["p"] = d["p"] + _dot(_mx(d["p"]), _mx(d["x"]))
        span *= 2
    for lm in level_masks:
        for key in chains:
            d = st[key]
            d["pb"] = _mx(d["p"])
            d["m"] = _dot(d["a_b"] * lm, d["pb"])
        for key in chains:
            d = st[key]
            d["p"] = d["p"] - _dot(d["pb"], _mx(d["m"]))
    for key in chains:
        d = st[key]
        e_g = jnp.exp(d["gcol"])
        rhs = jnp.concatenate([d["v"] * d["beta"], d["k"] * (d["beta"] * e_g)], axis=-1)
        sol = _dot(_mx(d["p"]), _mx(rhs))
        g_last = d["gcol"][C - 1:C, :]
        d.update(u=sol[:, :dk], wb=_mx(sol[:, dk:]), qgb=_mx(d["q"] * e_g),
                 kdb=_mx(d["k"] * jnp.exp(g_last - d["gcol"])), dl=jnp.exp(g_last))
    states = [jnp.where(first, 0.0, state_ref[h]) for h in range(B_HEADS)]
    for c in range(n_chunks):
        rs = slice(c * C, (c + 1) * C)
        for h in range(B_HEADS):
            d = st[c, h]
            sb = _mx(states[h])
            v_new = d["u"] - _dot(d["wb"], sb)
            vb = _mx(v_new)
            o = _dot(d["qgb"], sb) + _dot(d["attn_b"], vb)
            states[h] = states[h] * d["dl"] + _dot_tn(d["kdb"], vb)
            z = cur_ref[rs, wq + h * dk:wq + (h + 1) * dk]
            y = (o * lax.rsqrt(jnp.mean(o * o, -1, keepdims=True) + RMS_EPS) * nw_ref[...]
                 * _silu(z))
            o_ref[rs, h * dk:(h + 1) * dk] = y.astype(o_ref.dtype)
    for h in range(B_HEADS):
        state_ref[h] = states[h]


def _gdn(xb, w, cw, alog, dtb, nw, *, n_tiles, nt, ts):
    wq = cw.shape[1]
    wz = B_HEADS * V7X_LANES
    params = (cw, alog, dtb, nw)
    specs = [_const_spec(p.shape) for p in params]
    scratch = [
        pltpu.VMEM((CONV_PAD, wq), F32),
        pltpu.VMEM((B_HEADS, V7X_LANES, V7X_LANES), F32),
    ]
    return _mixer_call(functools.partial(_gdn_kernel, nt), xb, w, params, specs,
                       n_tiles=n_tiles, ts=ts, out_w=wz, scratch=scratch, name="gdn")


def _gla_kernel(nt, xb_ref, w_ref, wg_ref, bg_ref, nw_ref, o_ref, pn_ref, cur_ref, state_ref):
    ts = xb_ref.shape[0]
    C = GLA_CHUNK
    n_chunks = ts // C
    AT = GLA_ATTN_TILE
    kw = wg_ref.shape[1]
    dv = V7X_LANES
    wv = C_HEADS * dv
    pair_k = V7X_LANES
    dk = pair_k // 2
    n_pairs = kw // pair_k
    first = _lookahead_tile(nt)
    _zero_on_first_step(pn_ref, state_ref)
    _advance_projection(xb_ref, w_ref, pn_ref, cur_ref)
    sm = cur_ref[:, 2 * kw + 2 * wv:2 * kw + 2 * wv + SMALL_W]
    q = cur_ref[:, 0:kw] * np.float32(dk ** -0.5)
    k = cur_ref[:, kw:2 * kw]
    vb = _mx(cur_ref[:, 2 * kw:2 * kw + wv])

    zg = _dot(_mx(sm), wg_ref[...]) + bg_ref[...]
    log_a = -_softplus(-zg) * np.float32(1.0 / C_TAU)
    b_cum = _chunk_cumsum(log_a, C)
    q_in = q * jnp.exp(b_cum)
    kinb = _mx(k * jnp.exp(-b_cum))

    ri = _row_iota((AT, AT))
    ci = _col_iota((AT, AT))
    causal = jnp.logical_and(ci <= ri, (ri // C) == (ci // C))
    lane_head = _col_iota((AT, pair_k)) // dk
    bd_mask = (_row_iota((pair_k, 2 * dv)) // dk) == (_col_iota((pair_k, 2 * dv)) // dv)

    o_intra = {}
    for a in range(ts // AT):
        ra = slice(a * AT, (a + 1) * AT)
        for p in range(n_pairs):
            ks = slice(p * pair_k, (p + 1) * pair_k)
            for e in range(2):
                qm = _mx(jnp.where(lane_head == e, q_in[ra, ks], 0.0))
                attn = jnp.where(causal, _dot_nt(qm, kinb[ra, ks]), 0.0)
                hcol = slice((2 * p + e) * dv, (2 * p + e + 1) * dv)
                o_intra[a, 2 * p + e] = _dot(_mx(attn), vb[ra, hcol])
    o_inter = {}
    for p in range(n_pairs):
        ks = slice(p * pair_k, (p + 1) * pair_k)
        b_p = b_cum[:, ks]
        b_last_rows = [b_p[(c + 1) * C - 1:(c + 1) * C, :] for c in range(n_chunks)]
        b_last_blk = jnp.concatenate(
            b_last_rows + [jnp.zeros((V7X_LANES - n_chunks, pair_k), F32)], axis=0)
        d_last_cols = jnp.exp(b_last_blk).T
        upd = []
        for c in range(n_chunks):
            rs = slice(c * C, (c + 1) * C)
            k_dec = k[rs, ks] * jnp.exp(b_last_rows[c] - b_p[rs])
            upd.append(jnp.where(bd_mask, _dot_tn(_mx(k_dec), vb[rs, p * 2 * dv:(p + 1) * 2 * dv]), 0.0))
        state = jnp.where(first, 0.0, state_ref[p])
        qb_p = _mx(q_in[:, ks])
        for c in range(n_chunks):
            rs = slice(c * C, (c + 1) * C)
            o_inter[p, c] = _dot(qb_p[rs], _mx(state))
            state = state * d_last_cols[:, c:c + 1] + upd[c]
        state_ref[p] = state
    for p in range(n_pairs):
        o_int = jnp.concatenate([o_inter[p, c] for c in range(n_chunks)], axis=0)
        for e in range(2):
            hd = 2 * p + e
            hcol = slice(hd * dv, (hd + 1) * dv)
            o = (jnp.concatenate([o_intra[a, hd] for a in range(ts // AT)], axis=0)
                 + o_int[:, e * dv:(e + 1) * dv])
            r_gate = _silu(cur_ref[:, 2 * kw + wv + hd * dv:2 * kw + wv + (hd + 1) * dv])
            y = (o * lax.rsqrt(jnp.mean(o * o, -1, keepdims=True) + RMS_EPS) * nw_ref[...]
                 * r_gate)
            o_ref[:, hcol] = y.astype(o_ref.dtype)


def _gla(xb, w, wg, bg, nw, *, n_tiles, nt, ts):
    kw = wg.shape[1]
    wv = C_HEADS * V7X_LANES
    params = (wg, bg, nw)
    specs = [_const_spec(p.shape) for p in params]
    scratch = [
        pltpu.VMEM((kw // V7X_LANES, V7X_LANES, 2 * V7X_LANES), F32),
    ]
    return _mixer_call(functools.partial(_gla_kernel, nt), xb, w, params, specs,
                       n_tiles=n_tiles, ts=ts, out_w=wv, scratch=scratch, name="gla")


def _layer_norm(h, g, b):
    mu = jnp.mean(h, -1, keepdims=True)
    hc = h - mu
    var = jnp.mean(hc * hc, -1, keepdims=True)
    return hc * lax.rsqrt(var + LN_EPS) * g + b


def _merge_kernel(alpha, sub, ya_ref, yb_ref, yc_ref, xb_ref, x_ref, wm_ref, wb_ref, gb_ref,
                  wo_ref, g_ref, b_ref, o_ref):
    tm, D = x_ref.shape
    for r0 in range(0, tm, sub):
        rs = slice(r0, r0 + sub)
        xb = xb_ref[rs, :]
        merged = None
        for gi, y_ref in enumerate((ya_ref, yb_ref, yc_ref)):
            cs = slice(gi * D, (gi + 1) * D)
            gate = _sigmoid(_dot(xb, wm_ref[:, cs]) + gb_ref[:, cs])
            term = gate * _dot(y_ref[rs, :], wb_ref[gi])
            merged = term if merged is None else merged + term
        mix = _dot(_mx(merged), wo_ref[...])
        o_ref[rs, :] = _layer_norm(alpha * x_ref[rs, :] + mix, g_ref[...], b_ref[...])


def _merge_out(ya, yb, yc, xb, x, wm, wb, gb, wo, ln_g, ln_b, *, alpha, tm, sub):
    T, D = x.shape
    W = ya.shape[1]
    row = lambda w: pl.BlockSpec((tm, w), lambda i: (i, 0))
    return pl.pallas_call(
        functools.partial(_merge_kernel, alpha, sub),
        out_shape=jax.ShapeDtypeStruct((T, D), F32),
        grid=(T // tm,),
        in_specs=[
            row(W), row(W), row(W), row(D), row(D),
            pl.BlockSpec((D, N_BRANCH * D), lambda i: (0, 0)),
            pl.BlockSpec((N_BRANCH, W, D), lambda i: (0, 0, 0)),
            pl.BlockSpec((1, N_BRANCH * D), lambda i: (0, 0)),
            pl.BlockSpec((D, D), lambda i: (0, 0)),
            pl.BlockSpec((1, D), lambda i: (0, 0)),
            pl.BlockSpec((1, D), lambda i: (0, 0)),
        ],
        out_specs=row(D),
        compiler_params=pltpu.CompilerParams(
            dimension_semantics=("parallel",),
            vmem_limit_bytes=V7X_VMEM_LIMIT_BYTES),
        name="merge_out",
    )(ya, yb, yc, xb, x, wm, wb, gb, wo, ln_g, ln_b)


def _ffn_kernel(alpha, x_ref, w1_ref, w3_ref, w2_ref, g_ref, b_ref, o_ref, ob_ref, xb_ref, acc_ref):
    j = pl.program_id(1)

    @pl.when(j == 0)
    def _():
        xb_ref[...] = _mx(x_ref[...])

    xb = xb_ref[...]
    h = _silu(_dot(xb, w1_ref[...])) * _dot(xb, w3_ref[...])
    part = _dot(_mx(h), w2_ref[...])

    @pl.when(j == 0)
    def _():
        acc_ref[...] = part

    @pl.when(j > 0)
    def _():
        acc_ref[...] += part

    @pl.when(j == pl.num_programs(1) - 1)
    def _():
        y = _layer_norm(alpha * x_ref[...] + acc_ref[...], g_ref[...], b_ref[...])
        o_ref[...] = y
        ob_ref[...] = _mx(y)


def _ffn(x, w1, w3, w2, ln_g, ln_b, *, alpha, tm, tf):
    T, D = x.shape
    Fh = w1.shape[1]
    return pl.pallas_call(
        functools.partial(_ffn_kernel, alpha),
        out_shape=(jax.ShapeDtypeStruct((T, D), F32), jax.ShapeDtypeStruct((T, D), MXU_DTYPE)),
        grid=(T // tm, Fh // tf),
        in_specs=[
            pl.BlockSpec((tm, D), lambda i, j: (i, 0)),
            pl.BlockSpec((D, tf), lambda i, j: (0, j)),
            pl.BlockSpec((D, tf), lambda i, j: (0, j)),
            pl.BlockSpec((tf, D), lambda i, j: (j, 0)),
            pl.BlockSpec((1, D), lambda i, j: (0, 0)),
            pl.BlockSpec((1, D), lambda i, j: (0, 0)),
        ],
        out_specs=(pl.BlockSpec((tm, D), lambda i, j: (i, 0)),
                   pl.BlockSpec((tm, D), lambda i, j: (i, 0))),
        scratch_shapes=[pltpu.VMEM((tm, D), MXU_DTYPE), pltpu.VMEM((tm, D), F32)],
        compiler_params=pltpu.CompilerParams(
            dimension_semantics=("parallel", "arbitrary"),
            vmem_limit_bytes=V7X_VMEM_LIMIT_BYTES),
        name="ffn",
    )(x, w1, w3, w2, ln_g, ln_b)


def _tile(n, pref):
    t = min(n, pref)
    while n % t:
        t -= 1
    return t


def _pad_lanes(v, offset, width):
    return jnp.zeros((1, width), F32).at[0, offset:offset + v.shape[0]].set(v)


def kernel(x, w_in, a_conv_w, a_conv_b, a_w_r, a_b_r, a_w_i, a_b_i, a_lambda, b_conv_w, b_a_log,
           b_dt_bias, b_norm_w, c_w_g2, c_b_g2, c_norm_w, gate_b, w_branch, w_out, ln1_g, ln1_b,
           ffn_w1, ffn_w3, ffn_w2, ln2_g, ln2_b):
    B, S, D = x.shape
    depth = w_in.shape[0]
    T = B * S
    a_width = a_conv_w.shape[2]
    qkv_w = b_conv_w.shape[2]
    b_width = B_HEADS * V7X_LANES
    c_kw = c_w_g2.shape[2]
    c_rank = c_w_g2.shape[1]
    c_width = C_HEADS * V7X_LANES
    alpha = float((2.0 * depth) ** 0.25)

    splits = (a_width, a_width, qkv_w, b_width, B_HEADS, B_HEADS, c_kw, c_kw, c_width, c_rank,
              c_width, N_BRANCH * D)
    offs = np.concatenate([[0], np.cumsum(splits)])
    (s_ax, s_ag, s_qkv, s_z, s_beta, s_alpha, s_cq, s_ck, s_cv, s_cg, s_cr, s_merge) = [
        slice(int(offs[i]), int(offs[i + 1])) for i in range(len(splits))]
    n_small = 2 * B_HEADS + c_rank
    w_small = jnp.concatenate(
        [w_in[:, :, s_beta], w_in[:, :, s_alpha], w_in[:, :, s_cg],
         jnp.zeros((depth, D, SMALL_W - n_small), F32)], axis=-1)

    def cols(*groups):
        return jnp.concatenate([g if not isinstance(g, slice) else w_in[:, :, g] for g in groups],
                               axis=-1).astype(MXU_DTYPE)

    w_a = cols(s_ax, s_ag)
    w_b = cols(s_qkv, s_z, w_small)
    w_c = cols(s_cq, s_ck, s_cv, s_cr, w_small)
    w_m = cols(s_merge)

    eye_blocks = jnp.eye(A_BLOCKS, dtype=F32)

    def block_diag(w):
        return jnp.einsum("lhij,hg->lhigj", w, eye_blocks).reshape(depth, a_width, a_width)

    wr_bd = block_diag(a_w_r).astype(MXU_DTYPE)
    wi_bd = block_diag(a_w_i).astype(MXU_DTYPE)
    wg_pad = jnp.zeros((depth, SMALL_W, c_kw), F32).at[:, 2 * B_HEADS:n_small, :].set(c_w_g2)
    wg_pad = wg_pad.astype(MXU_DTYPE)
    wb = w_branch.astype(MXU_DTYPE)
    wo = w_out.astype(MXU_DTYPE)
    w1 = ffn_w1.astype(MXU_DTYPE)
    w3 = ffn_w3.astype(MXU_DTYPE)
    w2 = ffn_w2.astype(MXU_DTYPE)

    ts = _tile(S, 512)
    nt = S // ts
    n_tiles = B * nt
    assert ts % GDN_CHUNK == 0 and ts % GLA_ATTN_TILE == 0 and GLA_ATTN_TILE % GLA_CHUNK == 0
    tm_merge = _tile(T, 512)
    sub_merge = _tile(tm_merge, 256)
    tm_ffn = _tile(T, 512)
    tf_ffn = ffn_w1.shape[2] // 2
    assert tf_ffn % V7X_LANES == 0

    h = x.reshape(T, D)
    hb = _mx(h)
    for l in range(depth):
        ya = _rglru(hb, w_a[l], a_conv_w[l], a_conv_b[l][None], wr_bd[l], a_b_r[l][None],
                    wi_bd[l], a_b_i[l][None], a_lambda[l][None], n_tiles=n_tiles, nt=nt, ts=ts)
        yb = _gdn(hb, w_b[l], b_conv_w[l], _pad_lanes(b_a_log[l], B_HEADS, SMALL_W),
                  _pad_lanes(b_dt_bias[l], B_HEADS, SMALL_W), b_norm_w[l][None],
                  n_tiles=n_tiles, nt=nt, ts=ts)
        yc = _gla(hb, w_c[l], wg_pad[l], c_b_g2[l][None], c_norm_w[l][None],
                  n_tiles=n_tiles, nt=nt, ts=ts)
        h = _merge_out(ya, yb, yc, hb, h, w_m[l], wb[l], gate_b[l][None], wo[l], ln1_g[l][None],
                       ln1_b[l][None], alpha=alpha, tm=tm_merge, sub=sub_merge)
        h, hb = _ffn(h, w1[l], w3[l], w2[l], ln2_g[l][None], ln2_b[l][None],
                     alpha=alpha, tm=tm_ffn, tf=tf_ffn)
    return h.reshape(B, S, D)
```

```python
import functools

import jax
import jax.numpy as jnp
import numpy as np
from jax import lax
from jax.experimental import pallas as pl
from jax.experimental.pallas import tpu as pltpu

F32 = jnp.float32
MXU_DTYPE = jnp.bfloat16

V7X_LANES = 128
V7X_SUBLANES = 8
V7X_VMEM_LIMIT_BYTES = 56 * 1024 * 1024

N_BRANCH = 3
A_BLOCKS = 8
A_C = 8.0
CONV_K = 4
B_HEADS = 4
C_HEADS = 4
C_TAU = 16.0
GLA_CHUNK = 64
GLA_ATTN_TILE = 256
GDN_CHUNK = 256
INV_BASE = 8
PROJ_PIECE = 256
GLA_PIECES_UP_FRONT = 3
LN_EPS = 1e-5
RMS_EPS = 1e-6
L2_EPS = 1e-6
SMALL_W = V7X_LANES
CONV_PAD = V7X_SUBLANES


def _mx(x):
    return x.astype(MXU_DTYPE)


def _dot(a, b):
    return jnp.dot(a, b, preferred_element_type=F32)


def _dot_nt(a, b):
    return lax.dot_general(a, b, (((1,), (1,)), ((), ())), preferred_element_type=F32)


def _dot_tn(a, b):
    return lax.dot_general(a, b, (((0,), (0,)), ((), ())), preferred_element_type=F32)


def _sigmoid(x):
    return 1.0 / (1.0 + jnp.exp(-x))


def _silu(x):
    return x * _sigmoid(x)


def _softplus(x):
    return jnp.maximum(x, 0.0) + jnp.log1p(jnp.exp(-jnp.abs(x)))


def _gelu_tanh(x):
    c = np.float32(np.sqrt(2.0 / np.pi))
    return 0.5 * x * (1.0 + jnp.tanh(c * (x + 0.044715 * (x * x * x))))


def _row_iota(shape):
    return lax.broadcasted_iota(jnp.int32, shape, 0)


def _col_iota(shape):
    return lax.broadcasted_iota(jnp.int32, shape, 1)


def _chunk_cumsum(x, chunk):
    pos = _row_iota(x.shape) % chunk
    d = 1
    while d < chunk:
        x = x + jnp.where(pos >= d, pltpu.roll(x, d, 0), 0.0)
        d *= 2
    return x


def _causal_conv(hist_ref, x_ref, col0, width, w_ref, first_tile):
    ts = x_ref.shape[0]
    strips = []
    for c0 in range(0, width, V7X_LANES):
        cs = slice(c0, c0 + V7X_LANES)
        x = x_ref[:, col0 + c0:col0 + c0 + V7X_LANES]
        hist = jnp.where(first_tile, 0.0, hist_ref[:, cs])
        ext = jnp.concatenate([hist, x], axis=0)
        acc = x * w_ref[CONV_K - 1:CONV_K, cs]
        for back in range(1, CONV_K):
            shifted = pltpu.roll(ext, back, 0)[CONV_PAD:CONV_PAD + ts, :]
            acc = acc + shifted * w_ref[CONV_K - 1 - back:CONV_K - back, cs]
        hist_ref[:, cs] = x[ts - CONV_PAD:ts, :]
        strips.append(acc)
    return strips


def _lookahead_tile(nt):
    s = pl.program_id(0)
    tile = jnp.maximum(s - 1, 0)
    return lax.rem(tile, nt) == 0


def _zero_on_first_step(*refs):
    @pl.when(pl.program_id(0) == 0)
    def _():
        for ref in refs:
            ref[...] = jnp.zeros(ref.shape, ref.dtype)


class _Lookahead:
    def __init__(self, xb_ref, w_ref, pn_ref, cur_ref, piece_cols):
        cur_ref[...] = pn_ref[...]
        self._refs = (xb_ref, w_ref, pn_ref)
        width = w_ref.shape[1]
        self._todo = [(c0, min(c0 + piece_cols, width)) for c0 in range(0, width, piece_cols)]

    def issue(self, n=1):
        xb_ref, w_ref, pn_ref = self._refs
        for _ in range(min(n, len(self._todo))):
            c0, c1 = self._todo.pop(0)
            pn_ref[:, c0:c1] = _dot(xb_ref[...], w_ref[:, c0:c1])

    def finish(self):
        self.issue(len(self._todo))


def _mixer_call(body, xb, w, params, param_specs, *, n_tiles, ts, out_w, scratch, name):
    D = xb.shape[1]
    T = xb.shape[0]
    return pl.pallas_call(
        body,
        out_shape=jax.ShapeDtypeStruct((T, out_w), MXU_DTYPE),
        grid=(n_tiles + 1,),
        in_specs=[
            pl.BlockSpec((ts, D), lambda s: (jnp.minimum(s, n_tiles - 1), 0)),
            pl.BlockSpec(w.shape, lambda s: (0, 0)),
        ] + param_specs,
        out_specs=pl.BlockSpec((ts, out_w), lambda s: (jnp.maximum(s - 1, 0), 0)),
        scratch_shapes=[pltpu.VMEM((ts, w.shape[1]), F32),
                        pltpu.VMEM((ts, w.shape[1]), F32)
                        ] + scratch,
        compiler_params=pltpu.CompilerParams(
            dimension_semantics=("arbitrary",),
            vmem_limit_bytes=V7X_VMEM_LIMIT_BYTES),
        name=name,
    )(xb, w, *params)


def _const_spec(shape):
    return pl.BlockSpec(shape, lambda s: (0,) * len(shape))


def _rglru_kernel(nt, xb_ref, w_ref, cw_ref, cb_ref, wr_ref, br_ref, wi_ref, bi_ref, lam_ref,
                  o_ref, pn_ref, cur_ref, hist_ref, a_ref, u_ref, h_ref, carry_ref):
    ts = xb_ref.shape[0]
    W = cw_ref.shape[1]
    first = _lookahead_tile(nt)
    _zero_on_first_step(pn_ref, hist_ref, carry_ref)
    look = _Lookahead(xb_ref, w_ref, pn_ref, cur_ref, w_ref.shape[1])
    look.finish()
    xa = jnp.concatenate(_causal_conv(hist_ref, cur_ref, 0, W, cw_ref, first), axis=-1) + cb_ref[...]
    xab = _mx(xa)
    r = _sigmoid(_dot(xab, wr_ref[...]) + br_ref[...])
    i = _sigmoid(_dot(xab, wi_ref[...]) + bi_ref[...])
    log_a = (-A_C) * r * _softplus(-lam_ref[...])
    a = jnp.exp(log_a)
    a_ref[...] = a
    u_ref[...] = jnp.sqrt(1.0 - a * a) * (i * xa)
    rows = _row_iota((V7X_SUBLANES, W))
    h0 = jnp.where(first, 0.0, carry_ref[...])

    def body(blk, h_prev):
        r0 = pl.multiple_of(blk * V7X_SUBLANES, V7X_SUBLANES)
        ca = a_ref[pl.ds(r0, V7X_SUBLANES), :]
        cb = u_ref[pl.ds(r0, V7X_SUBLANES), :]
        d = 1
        while d < V7X_SUBLANES:
            keep = rows >= d
            cb = jnp.where(keep, ca * pltpu.roll(cb, d, 0) + cb, cb)
            ca = jnp.where(keep, ca * pltpu.roll(ca, d, 0), ca)
            d *= 2
        h = ca * h_prev + cb
        h_ref[pl.ds(r0, V7X_SUBLANES), :] = h
        return jnp.broadcast_to(h[V7X_SUBLANES - 1:V7X_SUBLANES, :], (V7X_SUBLANES, W))

    carry_ref[...] = lax.fori_loop(0, ts // V7X_SUBLANES, body, h0, unroll=4)
    o_ref[...] = (h_ref[...] * _gelu_tanh(cur_ref[:, W:2 * W])).astype(o_ref.dtype)


def _rglru(xb, w, cw, cb, wr, br, wi, bi, lam, *, n_tiles, nt, ts):
    W = cw.shape[1]
    params = (cw, cb, wr, br, wi, bi, lam)
    specs = [_const_spec(p.shape) for p in params]
    scratch = [
        pltpu.VMEM((CONV_PAD, W), F32),
        pltpu.VMEM((ts, W), F32),
        pltpu.VMEM((ts, W), F32),
        pltpu.VMEM((ts, W), F32),
        pltpu.VMEM((V7X_SUBLANES, W), F32),
    ]
    return _mixer_call(functools.partial(_rglru_kernel, nt), xb, w, params, specs,
                       n_tiles=n_tiles, ts=ts, out_w=W, scratch=scratch, name="rglru")


def _gdn_kernel(nt, xb_ref, w_ref, cw_ref, alog_ref, dtb_ref, nw_ref,
                o_ref, pn_ref, cur_ref, hist_ref, state_ref):
    ts = xb_ref.shape[0]
    C = GDN_CHUNK
    n_chunks = ts // C
    dk = V7X_LANES
    hw = B_HEADS * dk
    wq = cw_ref.shape[1]
    first = _lookahead_tile(nt)
    _zero_on_first_step(pn_ref, hist_ref, state_ref)
    look = _Lookahead(xb_ref, w_ref, pn_ref, cur_ref, w_ref.shape[1])
    look.finish()
    sm = cur_ref[:, wq + hw:wq + hw + SMALL_W]
    qkv = [_silu(s) for s in _causal_conv(hist_ref, cur_ref, 0, wq, cw_ref, first)]
    beta_all = _sigmoid(sm)
    g_all = -jnp.exp(alog_ref[...]) * _softplus(sm + dtb_ref[...])
    gcum_all = _chunk_cumsum(g_all, C)
    gcum_t = gcum_all.T

    ri = _row_iota((C, C))
    ci = _col_iota((C, C))
    lower = ci <= ri
    strict = ci < ri
    eye = (ci == ri).astype(F32)
    scale = np.float32(dk ** -0.5)
    base_mask = (ri // INV_BASE) == (ci // INV_BASE)
    level_masks = []
    n = INV_BASE
    while n < C:
        lm = jnp.logical_and((ri // (2 * n)) == (ci // (2 * n)), (ri // n) != (ci // n))
        level_masks.append(lm.astype(F32).astype(MXU_DTYPE))
        n *= 2

    chains = [(c, h) for c in range(n_chunks) for h in range(B_HEADS)]
    st = {}
    for c, h in chains:
        rs = slice(c * C, (c + 1) * C)
        q = qkv[h][rs]
        k = qkv[B_HEADS + h][rs]
        v = qkv[2 * B_HEADS + h][rs]
        q = q * lax.rsqrt(jnp.sum(q * q, -1, keepdims=True) + L2_EPS) * scale
        k = k * lax.rsqrt(jnp.sum(k * k, -1, keepdims=True) + L2_EPS)
        beta = beta_all[rs, h:h + 1]
        gcol = gcum_all[rs, B_HEADS + h:B_HEADS + h + 1]
        grow = gcum_t[B_HEADS + h:B_HEADS + h + 1, rs]
        decay = jnp.where(lower, jnp.exp(jnp.where(lower, gcol - grow, 0.0)), 0.0)
        qb, kb = _mx(q), _mx(k)
        kk = _dot_nt(kb, kb)
        attn_b = _mx(_dot_nt(qb, kb) * decay)
        a_mat = jnp.where(strict, beta * kk * decay, 0.0)
        a_base = jnp.where(base_mask, a_mat, 0.0)
        st[c, h] = dict(q=q, k=k, v=v, beta=beta, gcol=gcol, attn_b=attn_b, a_b=_mx(a_mat),
                        p=eye - a_base, x=a_base)
    span = 2
    while span < INV_BASE:
        for key in chains:
            d = st[key]
            xb = _mx(d["x"])
            d["x"] = _dot(xb, xb)
        for key in chains:
            d = st[key]
            d["p"] = d["p"] + _dot(_mx(d["p"]), _mx(d["x"]))
        span *= 2
    for lm in level_masks:
        for key in chains:
            d = st[key]
            d["pb"] = _mx(d["p"])
            d["m"] = _dot(d["a_b"] * lm, d["pb"])
        for key in chains:
            d = st[key]
            d["p"] = d["p"] - _dot(d["pb"], _mx(d["m"]))
    for key in chains:
        d = st[key]
        e_g = jnp.exp(d["gcol"])
        rhs = jnp.concatenate([d["v"] * d["beta"], d["k"] * (d["beta"] * e_g)], axis=-1)
        sol = _dot(_mx(d["p"]), _mx(rhs))
        g_last = d["gcol"][C - 1:C, :]
        d.update(u=sol[:, :dk], wb=_mx(sol[:, dk:]), qgb=_mx(d["q"] * e_g),
                 kdb=_mx(d["k"] * jnp.exp(g_last - d["gcol"])), dl=jnp.exp(g_last))
    states = [jnp.where(first, 0.0, state_ref[h]) for h in range(B_HEADS)]
    for c in range(n_chunks):
        rs = slice(c * C, (c + 1) * C)
        for h in range(B_HEADS):
            d = st[c, h]
            sb = _mx(states[h])
            v_new = d["u"] - _dot(d["wb"], sb)
            vb = _mx(v_new)
            o = _dot(d["qgb"], sb) + _dot(d["attn_b"], vb)
            states[h] = states[h] * d["dl"] + _dot_tn(d["kdb"], vb)
            z = cur_ref[rs, wq + h * dk:wq + (h + 1) * dk]
            y = (o * lax.rsqrt(jnp.mean(o * o, -1, keepdims=True) + RMS_EPS) * nw_ref[...]
                 * _silu(z))
            o_ref[rs, h * dk:(h + 1) * dk] = y.astype(o_ref.dtype)
    for h in range(B_HEADS):
        state_ref[h] = states[h]


def _gdn(xb, w, cw, alog, dtb, nw, *, n_tiles, nt, ts):
    wq = cw.shape[1]
    wz = B_HEADS * V7X_LANES
    params = (cw, alog, dtb, nw)
    specs = [_const_spec(p.shape) for p in params]
    scratch = [
        pltpu.VMEM((CONV_PAD, wq), F32),
        pltpu.VMEM((B_HEADS, V7X_LANES, V7X_LANES), F32),
    ]
    return _mixer_call(functools.partial(_gdn_kernel, nt), xb, w, params, specs,
                       n_tiles=n_tiles, ts=ts, out_w=wz, scratch=scratch, name="gdn")


def _gla_kernel(nt, xb_ref, w_ref, wg_ref, bg_ref, nw_ref, o_ref, pn_ref, cur_ref, state_ref):
    ts = xb_ref.shape[0]
    C = GLA_CHUNK
    n_chunks = ts // C
    AT = GLA_ATTN_TILE
    kw = wg_ref.shape[1]
    dv = V7X_LANES
    wv = C_HEADS * dv
    pair_k = V7X_LANES
    dk = pair_k // 2
    n_pairs = kw // pair_k
    first = _lookahead_tile(nt)
    _zero_on_first_step(pn_ref, state_ref)
    look = _Lookahead(xb_ref, w_ref, pn_ref, cur_ref, PROJ_PIECE)
    sm = cur_ref[:, 2 * kw + 2 * wv:2 * kw + 2 * wv + SMALL_W]
    q = cur_ref[:, 0:kw] * np.float32(dk ** -0.5)
    k = cur_ref[:, kw:2 * kw]
    vb = _mx(cur_ref[:, 2 * kw:2 * kw + wv])

    zg = _dot(_mx(sm), wg_ref[...]) + bg_ref[...]
    look.issue(GLA_PIECES_UP_FRONT)
    log_a = -_softplus(-zg) * np.float32(1.0 / C_TAU)
    b_cum = _chunk_cumsum(log_a, C)
    q_in = q * jnp.exp(b_cum)
    kinb = _mx(k * jnp.exp(-b_cum))

    ri = _row_iota((AT, AT))
    ci = _col_iota((AT, AT))
    causal = jnp.logical_and(ci <= ri, (ri // C) == (ci // C))
    lane_head = _col_iota((AT, pair_k)) // dk
    bd_mask = (_row_iota((pair_k, 2 * dv)) // dk) == (_col_iota((pair_k, 2 * dv)) // dv)

    o_intra = {}
    for a in range(ts // AT):
        ra = slice(a * AT, (a + 1) * AT)
        for p in range(n_pairs):
            ks = slice(p * pair_k, (p + 1) * pair_k)
            for e in range(2):
                qm = _mx(jnp.where(lane_head == e, q_in[ra, ks], 0.0))
                attn = jnp.where(causal, _dot_nt(qm, kinb[ra, ks]), 0.0)
                hcol = slice((2 * p + e) * dv, (2 * p + e + 1) * dv)
                o_intra[a, 2 * p + e] = _dot(_mx(attn), vb[ra, hcol])
    o_inter = {}
    for p in range(n_pairs):
        ks = slice(p * pair_k, (p + 1) * pair_k)
        b_p = b_cum[:, ks]
        b_last_rows = [b_p[(c + 1) * C - 1:(c + 1) * C, :] for c in range(n_chunks)]
        b_last_blk = jnp.concatenate(
            b_last_rows + [jnp.zeros((V7X_LANES - n_chunks, pair_k), F32)], axis=0)
        d_last_cols = jnp.exp(b_last_blk).T
        upd = []
        for c in range(n_chunks):
            rs = slice(c * C, (c + 1) * C)
            k_dec = k[rs, ks] * jnp.exp(b_last_rows[c] - b_p[rs])
            upd.append(jnp.where(bd_mask, _dot_tn(_mx(k_dec), vb[rs, p * 2 * dv:(p + 1) * 2 * dv]), 0.0))
        state = jnp.where(first, 0.0, state_ref[p])
        qb_p = _mx(q_in[:, ks])
        for c in range(n_chunks):
            rs = slice(c * C, (c + 1) * C)
            o_inter[p, c] = _dot(qb_p[rs], _mx(state))
            state = state * d_last_cols[:, c:c + 1] + upd[c]
        state_ref[p] = state
    look.finish()
    for p in range(n_pairs):
        o_int = jnp.concatenate([o_inter[p, c] for c in range(n_chunks)], axis=0)
        for e in range(2):
            hd = 2 * p + e
            hcol = slice(hd * dv, (hd + 1) * dv)
            o = (jnp.concatenate([o_intra[a, hd] for a in range(ts // AT)], axis=0)
                 + o_int[:, e * dv:(e + 1) * dv])
            r_gate = _silu(cur_ref[:, 2 * kw + wv + hd * dv:2 * kw + wv + (hd + 1) * dv])
            y = (o * lax.rsqrt(jnp.mean(o * o, -1, keepdims=True) + RMS_EPS) * nw_ref[...]
                 * r_gate)
            o_ref[:, hcol] = y.astype(o_ref.dtype)


def _gla(xb, w, wg, bg, nw, *, n_tiles, nt, ts):
    kw = wg.shape[1]
    wv = C_HEADS * V7X_LANES
    params = (wg, bg, nw)
    specs = [_const_spec(p.shape) for p in params]
    scratch = [
        pltpu.VMEM((kw // V7X_LANES, V7X_LANES, 2 * V7X_LANES), F32),
    ]
    return _mixer_call(functools.partial(_gla_kernel, nt), xb, w, params, specs,
                       n_tiles=n_tiles, ts=ts, out_w=wv, scratch=scratch, name="gla")


def _layer_norm(h, g, b):
    mu = jnp.mean(h, -1, keepdims=True)
    hc = h - mu
    var = jnp.mean(hc * hc, -1, keepdims=True)
    return hc * lax.rsqrt(var + LN_EPS) * g + b


def _merge_kernel(alpha, sub, ya_ref, yb_ref, yc_ref, xb_ref, x_ref, wm_ref, wb_ref, gb_ref,
                  wo_ref, g_ref, b_ref, o_ref, ob_ref):
    tm, D = x_ref.shape
    for r0 in range(0, tm, sub):
        rs = slice(r0, r0 + sub)
        xb = xb_ref[rs, :]
        merged = None
        for gi, y_ref in enumerate((ya_ref, yb_ref, yc_ref)):
            cs = slice(gi * D, (gi + 1) * D)
            gate = _sigmoid(_dot(xb, wm_ref[:, cs]) + gb_ref[:, cs])
            term = gate * _dot(y_ref[rs, :], wb_ref[gi])
            merged = term if merged is None else merged + term
        mix = _dot(_mx(merged), wo_ref[...])
        y = _layer_norm(alpha * x_ref[rs, :] + mix, g_ref[...], b_ref[...])
        o_ref[rs, :] = y
        ob_ref[rs, :] = _mx(y)


def _merge_out(ya, yb, yc, xb, x, wm, wb, gb, wo, ln_g, ln_b, *, alpha, tm, sub):
    T, D = x.shape
    W = ya.shape[1]
    row = lambda w: pl.BlockSpec((tm, w), lambda i: (i, 0))
    return pl.pallas_call(
        functools.partial(_merge_kernel, alpha, sub),
        out_shape=(jax.ShapeDtypeStruct((T, D), F32), jax.ShapeDtypeStruct((T, D), MXU_DTYPE)),
        grid=(T // tm,),
        in_specs=[
            row(W), row(W), row(W), row(D), row(D),
            pl.BlockSpec((D, N_BRANCH * D), lambda i: (0, 0)),
            pl.BlockSpec((N_BRANCH, W, D), lambda i: (0, 0, 0)),
            pl.BlockSpec((1, N_BRANCH * D), lambda i: (0, 0)),
            pl.BlockSpec((D, D), lambda i: (0, 0)),
            pl.BlockSpec((1, D), lambda i: (0, 0)),
            pl.BlockSpec((1, D), lambda i: (0, 0)),
        ],
        out_specs=(row(D), row(D)),
        compiler_params=pltpu.CompilerParams(
            dimension_semantics=("parallel",),
            vmem_limit_bytes=V7X_VMEM_LIMIT_BYTES),
        name="merge_out",
    )(ya, yb, yc, xb, x, wm, wb, gb, wo, ln_g, ln_b)


def _ffn_kernel(alpha, sub, x_ref, xb_ref, w1_ref, w3_ref, w2_ref, g_ref, b_ref, o_ref, ob_ref):
    tm = x_ref.shape[0]
    for r0 in range(0, tm, sub):
        rs = slice(r0, r0 + sub)
        xb = xb_ref[rs, :]
        h = _silu(_dot(xb, w1_ref[...])) * _dot(xb, w3_ref[...])
        y = _layer_norm(alpha * x_ref[rs, :] + _dot(_mx(h), w2_ref[...]), g_ref[...], b_ref[...])
        o_ref[rs, :] = y
        ob_ref[rs, :] = _mx(y)


def _ffn(x, xb, w1, w3, w2, ln_g, ln_b, *, alpha, tm, sub):
    T, D = x.shape
    row = pl.BlockSpec((tm, D), lambda i: (i, 0))
    whole = lambda a: pl.BlockSpec(a.shape, lambda i: (0,) * a.ndim)
    return pl.pallas_call(
        functools.partial(_ffn_kernel, alpha, sub),
        out_shape=(jax.ShapeDtypeStruct((T, D), F32), jax.ShapeDtypeStruct((T, D), MXU_DTYPE)),
        grid=(T // tm,),
        in_specs=[row, row, whole(w1), whole(w3), whole(w2), whole(ln_g), whole(ln_b)],
        out_specs=(row, row),
        compiler_params=pltpu.CompilerParams(
            dimension_semantics=("parallel",),
            vmem_limit_bytes=V7X_VMEM_LIMIT_BYTES),
        name="ffn",
    )(x, xb, w1, w3, w2, ln_g, ln_b)


def _tile(n, pref):
    t = min(n, pref)
    while n % t:
        t -= 1
    return t


def _pad_lanes(v, offset, width):
    return jnp.zeros((1, width), F32).at[0, offset:offset + v.shape[0]].set(v)


def kernel(x, w_in, a_conv_w, a_conv_b, a_w_r, a_b_r, a_w_i, a_b_i, a_lambda, b_conv_w, b_a_log,
           b_dt_bias, b_norm_w, c_w_g2, c_b_g2, c_norm_w, gate_b, w_branch, w_out, ln1_g, ln1_b,
           ffn_w1, ffn_w3, ffn_w2, ln2_g, ln2_b):
    B, S, D = x.shape
    depth = w_in.shape[0]
    T = B * S
    a_width = a_conv_w.shape[2]
    qkv_w = b_conv_w.shape[2]
    b_width = B_HEADS * V7X_LANES
    c_kw = c_w_g2.shape[2]
    c_rank = c_w_g2.shape[1]
    c_width = C_HEADS * V7X_LANES
    alpha = float((2.0 * depth) ** 0.25)

    splits = (a_width, a_width, qkv_w, b_width, B_HEADS, B_HEADS, c_kw, c_kw, c_width, c_rank,
              c_width, N_BRANCH * D)
    offs = np.concatenate([[0], np.cumsum(splits)])
    (s_ax, s_ag, s_qkv, s_z, s_beta, s_alpha, s_cq, s_ck, s_cv, s_cg, s_cr, s_merge) = [
        slice(int(offs[i]), int(offs[i + 1])) for i in range(len(splits))]
    n_small = 2 * B_HEADS + c_rank
    w_small = jnp.concatenate(
        [w_in[:, :, s_beta], w_in[:, :, s_alpha], w_in[:, :, s_cg],
         jnp.zeros((depth, D, SMALL_W - n_small), F32)], axis=-1)

    def cols(*groups):
        return jnp.concatenate([g if not isinstance(g, slice) else w_in[:, :, g] for g in groups],
                               axis=-1).astype(MXU_DTYPE)

    w_a = cols(s_ax, s_ag)
    w_b = cols(s_qkv, s_z, w_small)
    w_c = cols(s_cq, s_ck, s_cv, s_cr, w_small)
    w_m = cols(s_merge)

    eye_blocks = jnp.eye(A_BLOCKS, dtype=F32)

    def block_diag(w):
        return jnp.einsum("lhij,hg->lhigj", w, eye_blocks).reshape(depth, a_width, a_width)

    wr_bd = block_diag(a_w_r).astype(MXU_DTYPE)
    wi_bd = block_diag(a_w_i).astype(MXU_DTYPE)
    wg_pad = jnp.zeros((depth, SMALL_W, c_kw), F32).at[:, 2 * B_HEADS:n_small, :].set(c_w_g2)
    wg_pad = wg_pad.astype(MXU_DTYPE)
    wb = w_branch.astype(MXU_DTYPE)
    wo = w_out.astype(MXU_DTYPE)
    w1 = ffn_w1.astype(MXU_DTYPE)
    w3 = ffn_w3.astype(MXU_DTYPE)
    w2 = ffn_w2.astype(MXU_DTYPE)

    ts = _tile(S, 512)
    nt = S // ts
    n_tiles = B * nt
    assert ts % GDN_CHUNK == 0 and ts % GLA_ATTN_TILE == 0 and GLA_ATTN_TILE % GLA_CHUNK == 0
    tm_merge = _tile(T, 512)
    sub_merge = _tile(tm_merge, 256)
    tm_ffn = _tile(T, 1024)
    sub_ffn = _tile(tm_ffn, 256)

    h = x.reshape(T, D)
    hb = _mx(h)
    for l in range(depth):
        ya = _rglru(hb, w_a[l], a_conv_w[l], a_conv_b[l][None], wr_bd[l], a_b_r[l][None],
                    wi_bd[l], a_b_i[l][None], a_lambda[l][None], n_tiles=n_tiles, nt=nt, ts=ts)
        yb = _gdn(hb, w_b[l], b_conv_w[l], _pad_lanes(b_a_log[l], B_HEADS, SMALL_W),
                  _pad_lanes(b_dt_bias[l], B_HEADS, SMALL_W), b_norm_w[l][None],
                  n_tiles=n_tiles, nt=nt, ts=ts)
        yc = _gla(hb, w_c[l], wg_pad[l], c_b_g2[l][None], c_norm_w[l][None],
                  n_tiles=n_tiles, nt=nt, ts=ts)
        h, hb = _merge_out(ya, yb, yc, hb, h, w_m[l], wb[l], gate_b[l][None], wo[l],
                           ln1_g[l][None], ln1_b[l][None], alpha=alpha, tm=tm_merge, sub=sub_merge)
        h, hb = _ffn(h, hb, w1[l], w3[l], w2[l], ln2_g[l][None], ln2_b[l][None],
                     alpha=alpha, tm=tm_ffn, sub=sub_ffn)
    return h.reshape(B, S, D)
```

```python
import functools

import jax
import jax.numpy as jnp
import numpy as np
from jax import lax
from jax.experimental import pallas as pl
from jax.experimental.pallas import tpu as pltpu

F32 = jnp.float32
MXU_DTYPE = jnp.bfloat16

V7X_LANES = 128
V7X_SUBLANES = 8
V7X_VMEM_LIMIT_BYTES = 56 * 1024 * 1024

N_BRANCH = 3
A_BLOCKS = 8
A_C = 8.0
CONV_K = 4
B_HEADS = 4
C_HEADS = 4
C_TAU = 16.0
GLA_CHUNK = 64
GLA_ATTN_TILE = 256
GDN_CHUNK = 256
INV_BASE = 8
PROJ_PIECE = 256
GLA_PIECES_UP_FRONT = 3
LN_EPS = 1e-5
RMS_EPS = 1e-6
L2_EPS = 1e-6
SMALL_W = V7X_LANES
CONV_PAD = V7X_SUBLANES
OUT_ROWS = 2 * V7X_SUBLANES


def _mx(x):
    return x.astype(MXU_DTYPE)


def _dot(a, b):
    return jnp.dot(a, b, preferred_element_type=F32)


def _dot_nt(a, b):
    return lax.dot_general(a, b, (((1,), (1,)), ((), ())), preferred_element_type=F32)


def _dot_tn(a, b):
    return lax.dot_general(a, b, (((0,), (0,)), ((), ())), preferred_element_type=F32)


def _sigmoid(x):
    return 1.0 / (1.0 + jnp.exp(-x))


def _silu(x):
    return x * _sigmoid(x)


def _softplus(x):
    return jnp.maximum(x, 0.0) + jnp.log1p(jnp.exp(-jnp.abs(x)))


def _gelu_tanh(x):
    c = np.float32(np.sqrt(2.0 / np.pi))
    return 0.5 * x * (1.0 + jnp.tanh(c * (x + 0.044715 * (x * x * x))))


def _row_iota(shape):
    return lax.broadcasted_iota(jnp.int32, shape, 0)


def _col_iota(shape):
    return lax.broadcasted_iota(jnp.int32, shape, 1)


def _chunk_cumsum(x, chunk):
    pos = _row_iota(x.shape) % chunk
    d = 1
    while d < chunk:
        x = x + jnp.where(pos >= d, pltpu.roll(x, d, 0), 0.0)
        d *= 2
    return x


def _causal_conv(hist_ref, x_ref, col0, width, w_ref, first_tile):
    ts = x_ref.shape[0]
    strips = []
    for c0 in range(0, width, V7X_LANES):
        cs = slice(c0, c0 + V7X_LANES)
        x = x_ref[:, col0 + c0:col0 + c0 + V7X_LANES]
        hist = jnp.where(first_tile, 0.0, hist_ref[:, cs])
        ext = jnp.concatenate([hist, x], axis=0)
        acc = x * w_ref[CONV_K - 1:CONV_K, cs]
        for back in range(1, CONV_K):
            shifted = pltpu.roll(ext, back, 0)[CONV_PAD:CONV_PAD + ts, :]
            acc = acc + shifted * w_ref[CONV_K - 1 - back:CONV_K - back, cs]
        hist_ref[:, cs] = x[ts - CONV_PAD:ts, :]
        strips.append(acc)
    return strips


def _lookahead_tile(nt):
    s = pl.program_id(0)
    tile = jnp.maximum(s - 1, 0)
    return lax.rem(tile, nt) == 0


def _zero_on_first_step(*refs):
    @pl.when(pl.program_id(0) == 0)
    def _():
        for ref in refs:
            ref[...] = jnp.zeros(ref.shape, ref.dtype)


class _Lookahead:
    def __init__(self, xb_ref, w_ref, pn_ref, cur_ref, piece_cols):
        cur_ref[...] = pn_ref[...]
        self._refs = (xb_ref, w_ref, pn_ref)
        width = w_ref.shape[1]
        self._todo = [(c0, min(c0 + piece_cols, width)) for c0 in range(0, width, piece_cols)]

    def issue(self, n=1):
        xb_ref, w_ref, pn_ref = self._refs
        for _ in range(min(n, len(self._todo))):
            c0, c1 = self._todo.pop(0)
            pn_ref[:, c0:c1] = _dot(xb_ref[...], w_ref[:, c0:c1])

    def finish(self):
        self.issue(len(self._todo))


def _mixer_call(body, xb, w, params, *, layer, n_tiles, ts, out_w, scratch, name):
    D = xb.shape[1]
    T = xb.shape[0]
    return pl.pallas_call(
        body,
        out_shape=jax.ShapeDtypeStruct((T, out_w), MXU_DTYPE),
        grid=(n_tiles + 1,),
        in_specs=[
            pl.BlockSpec((ts, D), lambda s: (jnp.minimum(s, n_tiles - 1), 0)),
            _layer_spec(w, layer),
        ] + [_layer_spec(p, layer) for p in params],
        out_specs=pl.BlockSpec((ts, out_w), lambda s: (jnp.maximum(s - 1, 0), 0)),
        scratch_shapes=[pltpu.VMEM((ts, w.shape[-1]), F32),
                        pltpu.VMEM((ts, w.shape[-1]), F32)
                        ] + scratch,
        compiler_params=pltpu.CompilerParams(
            dimension_semantics=("arbitrary",),
            vmem_limit_bytes=V7X_VMEM_LIMIT_BYTES),
        name=name,
    )(xb, w, *params)


def _layer_spec(a, layer):
    return pl.BlockSpec((None,) + a.shape[1:], lambda *_: (layer,) + (0,) * (a.ndim - 1))


def _rglru_kernel(nt, xb_ref, w_ref, cw_ref, cb_ref, wr_ref, br_ref, wi_ref, bi_ref, lam_ref,
                  o_ref, pn_ref, cur_ref, hist_ref, a_ref, u_ref, carry_ref):
    ts = xb_ref.shape[0]
    W = cw_ref.shape[1]
    first = _lookahead_tile(nt)
    _zero_on_first_step(pn_ref, hist_ref, carry_ref)
    look = _Lookahead(xb_ref, w_ref, pn_ref, cur_ref, w_ref.shape[1])
    look.finish()
    xa = jnp.concatenate(_causal_conv(hist_ref, cur_ref, 0, W, cw_ref, first), axis=-1) + cb_ref[...]
    xab = _mx(xa)
    r = _sigmoid(_dot(xab, wr_ref[...]) + br_ref[...])
    i = _sigmoid(_dot(xab, wi_ref[...]) + bi_ref[...])
    log_a = (-A_C) * r * _softplus(-lam_ref[...])
    a = jnp.exp(log_a)
    a_ref[...] = a
    u_ref[...] = jnp.sqrt(1.0 - a * a) * (i * xa)
    rows = _row_iota((V7X_SUBLANES, W))
    h0 = jnp.where(first, 0.0, carry_ref[...])

    def scan8(r0, h_prev):
        ca = a_ref[pl.ds(r0, V7X_SUBLANES), :]
        cb = u_ref[pl.ds(r0, V7X_SUBLANES), :]
        d = 1
        while d < V7X_SUBLANES:
            keep = rows >= d
            cb = jnp.where(keep, ca * pltpu.roll(cb, d, 0) + cb, cb)
            ca = jnp.where(keep, ca * pltpu.roll(ca, d, 0), ca)
            d *= 2
        h = ca * h_prev + cb
        return h, jnp.broadcast_to(h[V7X_SUBLANES - 1:V7X_SUBLANES, :], (V7X_SUBLANES, W))

    def body(blk, h_prev):
        r0 = pl.multiple_of(blk * OUT_ROWS, OUT_ROWS)
        hs = []
        for j in range(OUT_ROWS // V7X_SUBLANES):
            h, h_prev = scan8(r0 + j * V7X_SUBLANES, h_prev)
            hs.append(h)
        gate = _gelu_tanh(cur_ref[pl.ds(r0, OUT_ROWS), W:2 * W])
        o_ref[pl.ds(r0, OUT_ROWS), :] = (jnp.concatenate(hs, axis=0) * gate).astype(o_ref.dtype)
        return h_prev

    carry_ref[...] = lax.fori_loop(0, ts // OUT_ROWS, body, h0, unroll=2)


def _rglru(xb, w, cw, cb, wr, br, wi, bi, lam, *, layer, n_tiles, nt, ts):
    W = cw.shape[-1]
    params = (cw, cb, wr, br, wi, bi, lam)
    scratch = [
        pltpu.VMEM((CONV_PAD, W), F32),
        pltpu.VMEM((ts, W), F32),
        pltpu.VMEM((ts, W), F32),
        pltpu.VMEM((V7X_SUBLANES, W), F32),
    ]
    return _mixer_call(functools.partial(_rglru_kernel, nt), xb, w, params, layer=layer,
                       n_tiles=n_tiles, ts=ts, out_w=W, scratch=scratch, name="rglru")


def _gdn_kernel(nt, xb_ref, w_ref, cw_ref, alog_ref, dtb_ref, nw_ref,
                o_ref, pn_ref, cur_ref, hist_ref, state_ref):
    ts = xb_ref.shape[0]
    C = GDN_CHUNK
    n_chunks = ts // C
    dk = V7X_LANES
    hw = B_HEADS * dk
    wq = cw_ref.shape[1]
    first = _lookahead_tile(nt)
    _zero_on_first_step(pn_ref, hist_ref, state_ref)
    look = _Lookahead(xb_ref, w_ref, pn_ref, cur_ref, w_ref.shape[1])
    look.finish()
    sm = cur_ref[:, wq + hw:wq + hw + SMALL_W]
    qkv = [_silu(s) for s in _causal_conv(hist_ref, cur_ref, 0, wq, cw_ref, first)]
    beta_all = _sigmoid(sm)
    g_all = -jnp.exp(alog_ref[...]) * _softplus(sm + dtb_ref[...])
    gcum_all = _chunk_cumsum(g_all, C)
    gcum_t = gcum_all.T

    ri = _row_iota((C, C))
    ci = _col_iota((C, C))
    lower = ci <= ri
    strict = ci < ri
    eye = (ci == ri).astype(F32)
    scale = np.float32(dk ** -0.5)
    base_mask = (ri // INV_BASE) == (ci // INV_BASE)
    level_masks = []
    n = INV_BASE
    while n < C:
        lm = jnp.logical_and((ri // (2 * n)) == (ci // (2 * n)), (ri // n) != (ci // n))
        level_masks.append(lm.astype(F32).astype(MXU_DTYPE))
        n *= 2

    chains = [(c, h) for c in range(n_chunks) for h in range(B_HEADS)]
    st = {}
    for c, h in chains:
        rs = slice(c * C, (c + 1) * C)
        q = qkv[h][rs]
        k = qkv[B_HEADS + h][rs]
        v = qkv[2 * B_HEADS + h][rs]
        q = q * lax.rsqrt(jnp.sum(q * q, -1, keepdims=True) + L2_EPS) * scale
        k = k * lax.rsqrt(jnp.sum(k * k, -1, keepdims=True) + L2_EPS)
        beta = beta_all[rs, h:h + 1]
        gcol = gcum_all[rs, B_HEADS + h:B_HEADS + h + 1]
        grow = gcum_t[B_HEADS + h:B_HEADS + h + 1, rs]
        decay = jnp.where(lower, jnp.exp(jnp.where(lower, gcol - grow, 0.0)), 0.0)
        qb, kb = _mx(q), _mx(k)
        kk = _dot_nt(kb, kb)
        attn_b = _mx(_dot_nt(qb, kb) * decay)
        a_mat = jnp.where(strict, beta * kk * decay, 0.0)
        a_base = jnp.where(base_mask, a_mat, 0.0)
        st[c, h] = dict(q=q, k=k, v=v, beta=beta, gcol=gcol, attn_b=attn_b, a_b=_mx(a_mat),
                        p=eye - a_base, x=a_base)
    span = 2
    while span < INV_BASE:
        for key in chains:
            d = st[key]
            xb = _mx(d["x"])
            d["x"] = _dot(xb, xb)
        for key in chains:
            d = st[key]
            d["p"] = d["p"] + _dot(_mx(d["p"]), _mx(d["x"]))
        span *= 2
    for lm in level_masks:
        for key in chains:
            d = st[key]
            d["pb"] = _mx(d["p"])
            d["m"] = _dot(d["a_b"] * lm, d["pb"])
        for key in chains:
            d = st[key]
            d["p"] = d["p"] - _dot(d["pb"], _mx(d["m"]))
    for key in chains:
        d = st[key]
        e_g = jnp.exp(d["gcol"])
        rhs = jnp.concatenate([d["v"] * d["beta"], d["k"] * (d["beta"] * e_g)], axis=-1)
        sol = _dot(_mx(d["p"]), _mx(rhs))
        g_last = d["gcol"][C - 1:C, :]
        d.update(u=sol[:, :dk], wb=_mx(sol[:, dk:]), qgb=_mx(d["q"] * e_g),
                 kdb=_mx(d["k"] * jnp.exp(g_last - d["gcol"])), dl=jnp.exp(g_last))
    states = [jnp.where(first, 0.0, state_ref[h]) for h in range(B_HEADS)]
    for c in range(n_chunks):
        rs = slice(c * C, (c + 1) * C)
        for h in range(B_HEADS):
            d = st[c, h]
            sb = _mx(states[h])
            v_new = d["u"] - _dot(d["wb"], sb)
            vb = _mx(v_new)
            o = _dot(d["qgb"], sb) + _dot(d["attn_b"], vb)
            states[h] = states[h] * d["dl"] + _dot_tn(d["kdb"], vb)
            z = cur_ref[rs, wq + h * dk:wq + (h + 1) * dk]
            y = (o * lax.rsqrt(jnp.mean(o * o, -1, keepdims=True) + RMS_EPS) * nw_ref[...]
                 * _silu(z))
            o_ref[rs, h * dk:(h + 1) * dk] = y.astype(o_ref.dtype)
    for h in range(B_HEADS):
        state_ref[h] = states[h]


def _gdn(xb, w, cw, alog, dtb, nw, *, layer, n_tiles, nt, ts):
    wq = cw.shape[-1]
    wz = B_HEADS * V7X_LANES
    params = (cw, alog, dtb, nw)
    scratch = [
        pltpu.VMEM((CONV_PAD, wq), F32),
        pltpu.VMEM((B_HEADS, V7X_LANES, V7X_LANES), F32),
    ]
    return _mixer_call(functools.partial(_gdn_kernel, nt), xb, w, params, layer=layer,
                       n_tiles=n_tiles, ts=ts, out_w=wz, scratch=scratch, name="gdn")


def _gla_kernel(nt, xb_ref, w_ref, wg_ref, bg_ref, nw_ref, o_ref, pn_ref, cur_ref, state_ref):
    ts = xb_ref.shape[0]
    C = GLA_CHUNK
    n_chunks = ts // C
    AT = GLA_ATTN_TILE
    kw = wg_ref.shape[1]
    dv = V7X_LANES
    wv = C_HEADS * dv
    pair_k = V7X_LANES
    dk = pair_k // 2
    n_pairs = kw // pair_k
    first = _lookahead_tile(nt)
    _zero_on_first_step(pn_ref, state_ref)
    look = _Lookahead(xb_ref, w_ref, pn_ref, cur_ref, PROJ_PIECE)
    sm = cur_ref[:, 2 * kw + 2 * wv:2 * kw + 2 * wv + SMALL_W]
    q = cur_ref[:, 0:kw] * np.float32(dk ** -0.5)
    k = cur_ref[:, kw:2 * kw]
    vb = _mx(cur_ref[:, 2 * kw:2 * kw + wv])

    zg = _dot(_mx(sm), wg_ref[...]) + bg_ref[...]
    look.issue(GLA_PIECES_UP_FRONT)
    log_a = -_softplus(-zg) * np.float32(1.0 / C_TAU)
    b_cum = _chunk_cumsum(log_a, C)
    q_in = q * jnp.exp(b_cum)
    kinb = _mx(k * jnp.exp(-b_cum))

    ri = _row_iota((AT, AT))
    ci = _col_iota((AT, AT))
    causal = jnp.logical_and(ci <= ri, (ri // C) == (ci // C))
    lane_head = _col_iota((AT, pair_k)) // dk
    bd_mask = (_row_iota((pair_k, 2 * dv)) // dk) == (_col_iota((pair_k, 2 * dv)) // dv)

    o_intra = {}
    for a in range(ts // AT):
        ra = slice(a * AT, (a + 1) * AT)
        for p in range(n_pairs):
            ks = slice(p * pair_k, (p + 1) * pair_k)
            for e in range(2):
                qm = _mx(jnp.where(lane_head == e, q_in[ra, ks], 0.0))
                attn = jnp.where(causal, _dot_nt(qm, kinb[ra, ks]), 0.0)
                hcol = slice((2 * p + e) * dv, (2 * p + e + 1) * dv)
                o_intra[a, 2 * p + e] = _dot(_mx(attn), vb[ra, hcol])
    o_inter = {}
    for p in range(n_pairs):
        ks = slice(p * pair_k, (p + 1) * pair_k)
        b_p = b_cum[:, ks]
        b_last_rows = [b_p[(c + 1) * C - 1:(c + 1) * C, :] for c in range(n_chunks)]
        b_last_blk = jnp.concatenate(
            b_last_rows + [jnp.zeros((V7X_LANES - n_chunks, pair_k), F32)], axis=0)
        d_last_cols = jnp.exp(b_last_blk).T
        upd = []
        for c in range(n_chunks):
            rs = slice(c * C, (c + 1) * C)
            k_dec = k[rs, ks] * jnp.exp(b_last_rows[c] - b_p[rs])
            upd.append(jnp.where(bd_mask, _dot_tn(_mx(k_dec), vb[rs, p * 2 * dv:(p + 1) * 2 * dv]), 0.0))
        state = jnp.where(first, 0.0, state_ref[p])
        qb_p = _mx(q_in[:, ks])
        for c in range(n_chunks):
            rs = slice(c * C, (c + 1) * C)
            o_inter[p, c] = _dot(qb_p[rs], _mx(state))
            state = state * d_last_cols[:, c:c + 1] + upd[c]
        state_ref[p] = state
    look.finish()
    for p in range(n_pairs):
        o_int = jnp.concatenate([o_inter[p, c] for c in range(n_chunks)], axis=0)
        for e in range(2):
            hd = 2 * p + e
            hcol = slice(hd * dv, (hd + 1) * dv)
            o = (jnp.concatenate([o_intra[a, hd] for a in range(ts // AT)], axis=0)
                 + o_int[:, e * dv:(e + 1) * dv])
            r_gate = _silu(cur_ref[:, 2 * kw + wv + hd * dv:2 * kw + wv + (hd + 1) * dv])
            y = (o * lax.rsqrt(jnp.mean(o * o, -1, keepdims=True) + RMS_EPS) * nw_ref[...]
                 * r_gate)
            o_ref[:, hcol] = y.astype(o_ref.dtype)


def _gla(xb, w, wg, bg, nw, *, layer, n_tiles, nt, ts):
    kw = wg.shape[-1]
    wv = C_HEADS * V7X_LANES
    params = (wg, bg, nw)
    scratch = [
        pltpu.VMEM((kw // V7X_LANES, V7X_LANES, 2 * V7X_LANES), F32),
    ]
    return _mixer_call(functools.partial(_gla_kernel, nt), xb, w, params, layer=layer,
                       n_tiles=n_tiles, ts=ts, out_w=wv, scratch=scratch, name="gla")


def _layer_norm(h, g, b):
    mu = jnp.mean(h, -1, keepdims=True)
    hc = h - mu
    var = jnp.mean(hc * hc, -1, keepdims=True)
    return hc * lax.rsqrt(var + LN_EPS) * g + b


def _merge_kernel(alpha, sub, ya_ref, yb_ref, yc_ref, xb_ref, x_ref, wm_ref, wb_ref, gb_ref,
                  wo_ref, g_ref, b_ref, o_ref, ob_ref):
    tm, D = x_ref.shape
    for r0 in range(0, tm, sub):
        rs = slice(r0, r0 + sub)
        xb = xb_ref[rs, :]
        merged = None
        for gi, y_ref in enumerate((ya_ref, yb_ref, yc_ref)):
            cs = slice(gi * D, (gi + 1) * D)
            gate = _sigmoid(_dot(xb, wm_ref[:, cs]) + gb_ref[:, cs])
            term = gate * _dot(y_ref[rs, :], wb_ref[gi])
            merged = term if merged is None else merged + term
        mix = _dot(_mx(merged), wo_ref[...])
        y = _layer_norm(alpha * x_ref[rs, :] + mix, g_ref[...], b_ref[...])
        o_ref[rs, :] = y
        ob_ref[rs, :] = _mx(y)


def _merge_out(ya, yb, yc, xb, x, wm, wb, gb, wo, ln_g, ln_b, *, layer, alpha, tm, sub):
    T, D = x.shape
    W = ya.shape[1]
    row = lambda w: pl.BlockSpec((tm, w), lambda i: (i, 0))
    params = (wm, wb, gb, wo, ln_g, ln_b)
    return pl.pallas_call(
        functools.partial(_merge_kernel, alpha, sub),
        out_shape=(jax.ShapeDtypeStruct((T, D), F32), jax.ShapeDtypeStruct((T, D), MXU_DTYPE)),
        grid=(T // tm,),
        in_specs=[row(W), row(W), row(W), row(D), row(D)] + [_layer_spec(p, layer) for p in params],
        out_specs=(row(D), row(D)),
        compiler_params=pltpu.CompilerParams(
            dimension_semantics=("parallel",),
            vmem_limit_bytes=V7X_VMEM_LIMIT_BYTES),
        name="merge_out",
    )(ya, yb, yc, xb, x, wm, wb, gb, wo, ln_g, ln_b)


def _ffn_kernel(alpha, sub, x_ref, xb_ref, w1_ref, w3_ref, w2_ref, g_ref, b_ref, o_ref, ob_ref):
    tm = x_ref.shape[0]
    for r0 in range(0, tm, sub):
        rs = slice(r0, r0 + sub)
        xb = xb_ref[rs, :]
        h = _silu(_dot(xb, w1_ref[...])) * _dot(xb, w3_ref[...])
        y = _layer_norm(alpha * x_ref[rs, :] + _dot(_mx(h), w2_ref[...]), g_ref[...], b_ref[...])
        o_ref[rs, :] = y
        ob_ref[rs, :] = _mx(y)


def _ffn(x, xb, w1, w3, w2, ln_g, ln_b, *, layer, alpha, tm, sub):
    T, D = x.shape
    row = pl.BlockSpec((tm, D), lambda i: (i, 0))
    return pl.pallas_call(
        functools.partial(_ffn_kernel, alpha, sub),
        out_shape=(jax.ShapeDtypeStruct((T, D), F32), jax.ShapeDtypeStruct((T, D), MXU_DTYPE)),
        grid=(T // tm,),
        in_specs=[row, row] + [_layer_spec(p, layer) for p in (w1, w3, w2, ln_g, ln_b)],
        out_specs=(row, row),
        compiler_params=pltpu.CompilerParams(
            dimension_semantics=("parallel",),
            vmem_limit_bytes=V7X_VMEM_LIMIT_BYTES),
        name="ffn",
    )(x, xb, w1, w3, w2, ln_g, ln_b)


def _tile(n, pref):
    t = min(n, pref)
    while n % t:
        t -= 1
    return t


def _pad_lanes(v, offset, width):
    return jnp.zeros((v.shape[0], 1, width), F32).at[:, 0, offset:offset + v.shape[1]].set(v)


def kernel(x, w_in, a_conv_w, a_conv_b, a_w_r, a_b_r, a_w_i, a_b_i, a_lambda, b_conv_w, b_a_log,
           b_dt_bias, b_norm_w, c_w_g2, c_b_g2, c_norm_w, gate_b, w_branch, w_out, ln1_g, ln1_b,
           ffn_w1, ffn_w3, ffn_w2, ln2_g, ln2_b):
    B, S, D = x.shape
    depth = w_in.shape[0]
    T = B * S
    a_width = a_conv_w.shape[2]
    qkv_w = b_conv_w.shape[2]
    b_width = B_HEADS * V7X_LANES
    c_kw = c_w_g2.shape[2]
    c_rank = c_w_g2.shape[1]
    c_width = C_HEADS * V7X_LANES
    alpha = float((2.0 * depth) ** 0.25)

    splits = (a_width, a_width, qkv_w, b_width, B_HEADS, B_HEADS, c_kw, c_kw, c_width, c_rank,
              c_width, N_BRANCH * D)
    offs = np.concatenate([[0], np.cumsum(splits)])
    (s_ax, s_ag, s_qkv, s_z, s_beta, s_alpha, s_cq, s_ck, s_cv, s_cg, s_cr, s_merge) = [
        slice(int(offs[i]), int(offs[i + 1])) for i in range(len(splits))]
    n_small = 2 * B_HEADS + c_rank
    w_in_b = _mx(w_in)
    w_small = jnp.concatenate(
        [w_in_b[:, :, s_beta], w_in_b[:, :, s_alpha], w_in_b[:, :, s_cg],
         jnp.zeros((depth, D, SMALL_W - n_small), MXU_DTYPE)], axis=-1)

    def cols(*groups):
        return jnp.concatenate([g if not isinstance(g, slice) else w_in_b[:, :, g] for g in groups],
                               axis=-1)

    w_a = cols(s_ax, s_ag)
    w_b = cols(s_qkv, s_z, w_small)
    w_c = cols(s_cq, s_ck, s_cv, s_cr, w_small)
    w_m = cols(s_merge)

    eye_blocks = jnp.eye(A_BLOCKS, dtype=F32)

    def block_diag(w):
        return jnp.einsum("lhij,hg->lhigj", w, eye_blocks).reshape(depth, a_width, a_width)

    wr_bd = block_diag(a_w_r).astype(MXU_DTYPE)
    wi_bd = block_diag(a_w_i).astype(MXU_DTYPE)
    wg_pad = jnp.zeros((depth, SMALL_W, c_kw), F32).at[:, 2 * B_HEADS:n_small, :].set(c_w_g2)
    wg_pad = wg_pad.astype(MXU_DTYPE)
    wb = w_branch.astype(MXU_DTYPE)
    wo = w_out.astype(MXU_DTYPE)
    w1 = ffn_w1.astype(MXU_DTYPE)
    w3 = ffn_w3.astype(MXU_DTYPE)
    w2 = ffn_w2.astype(MXU_DTYPE)

    ts = _tile(S, 512)
    nt = S // ts
    n_tiles = B * nt
    assert ts % GDN_CHUNK == 0 and ts % GLA_ATTN_TILE == 0 and GLA_ATTN_TILE % GLA_CHUNK == 0
    tm_merge = _tile(T, 512)
    sub_merge = _tile(tm_merge, 256)
    tm_ffn = _tile(T, 1024)
    sub_ffn = _tile(tm_ffn, 256)

    row = lambda v: v[:, None, :]
    alog_pad = _pad_lanes(b_a_log, B_HEADS, SMALL_W)
    dtb_pad = _pad_lanes(b_dt_bias, B_HEADS, SMALL_W)
    mix = dict(n_tiles=n_tiles, nt=nt, ts=ts)

    h = x.reshape(T, D)
    hb = _mx(h)
    for l in range(depth):
        ya = _rglru(hb, w_a, a_conv_w, row(a_conv_b), wr_bd, row(a_b_r), wi_bd, row(a_b_i),
                    row(a_lambda), layer=l, **mix)
        yb = _gdn(hb, w_b, b_conv_w, alog_pad, dtb_pad, row(b_norm_w), layer=l, **mix)
        yc = _gla(hb, w_c, wg_pad, row(c_b_g2), row(c_norm_w), layer=l, **mix)
        h, hb = _merge_out(ya, yb, yc, hb, h, w_m, wb, row(gate_b), wo, row(ln1_g), row(ln1_b),
                           layer=l, alpha=alpha, tm=tm_merge, sub=sub_merge)
        h, hb = _ffn(h, hb, w1, w3, w2, row(ln2_g), row(ln2_b),
                     layer=l, alpha=alpha, tm=tm_ffn, sub=sub_ffn)
    return h.reshape(B, S, D)
```

```python
import functools

import jax
import jax.numpy as jnp
import numpy as np
from jax import lax
from jax.experimental import pallas as pl
from jax.experimental.pallas import tpu as pltpu

F32 = jnp.float32
MXU_DTYPE = jnp.bfloat16

V7X_LANES = 128
V7X_SUBLANES = 8
V7X_VMEM_LIMIT_BYTES = 56 * 1024 * 1024

N_BRANCH = 3
A_BLOCKS = 8
A_C = 8.0
CONV_K = 4
B_HEADS = 4
C_HEADS = 4
C_TAU = 16.0
GLA_CHUNK = 64
GLA_ATTN_TILE = 256
GDN_CHUNK = 128
INV_BASE = 8
PROJ_PIECE = 256
GLA_PIECES_UP_FRONT = 3
LN_EPS = 1e-5
RMS_EPS = 1e-6
L2_EPS = 1e-6
SMALL_W = V7X_LANES
CONV_PAD = V7X_SUBLANES
OUT_ROWS = 2 * V7X_SUBLANES


def _mx(x):
    return x.astype(MXU_DTYPE)


def _dot(a, b):
    return jnp.dot(a, b, preferred_element_type=F32)


def _dot_nt(a, b):
    return lax.dot_general(a, b, (((1,), (1,)), ((), ())), preferred_element_type=F32)


def _dot_tn(a, b):
    return lax.dot_general(a, b, (((0,), (0,)), ((), ())), preferred_element_type=F32)


def _sigmoid(x):
    return 1.0 / (1.0 + jnp.exp(-x))


def _silu(x):
    return x * _sigmoid(x)


def _softplus(x):
    return jnp.maximum(x, 0.0) + jnp.log1p(jnp.exp(-jnp.abs(x)))


def _gelu_tanh(x):
    c = np.float32(np.sqrt(2.0 / np.pi))
    return 0.5 * x * (1.0 + jnp.tanh(c * (x + 0.044715 * (x * x * x))))


def _row_iota(shape):
    return lax.broadcasted_iota(jnp.int32, shape, 0)


def _col_iota(shape):
    return lax.broadcasted_iota(jnp.int32, shape, 1)


def _chunk_cumsum(x, chunk):
    pos = _row_iota(x.shape) % chunk
    d = 1
    while d < chunk:
        x = x + jnp.where(pos >= d, pltpu.roll(x, d, 0), 0.0)
        d *= 2
    return x


def _causal_conv(hist_ref, x_ref, col0, width, w_ref, first_tile):
    ts = x_ref.shape[0]
    strips = []
    for c0 in range(0, width, V7X_LANES):
        cs = slice(c0, c0 + V7X_LANES)
        x = x_ref[:, col0 + c0:col0 + c0 + V7X_LANES]
        hist = jnp.where(first_tile, 0.0, hist_ref[:, cs])
        ext = jnp.concatenate([hist, x], axis=0)
        acc = x * w_ref[CONV_K - 1:CONV_K, cs]
        for back in range(1, CONV_K):
            shifted = pltpu.roll(ext, back, 0)[CONV_PAD:CONV_PAD + ts, :]
            acc = acc + shifted * w_ref[CONV_K - 1 - back:CONV_K - back, cs]
        hist_ref[:, cs] = x[ts - CONV_PAD:ts, :]
        strips.append(acc)
    return strips


def _lookahead_tile(nt):
    s = pl.program_id(0)
    tile = jnp.maximum(s - 1, 0)
    return lax.rem(tile, nt) == 0


def _zero_on_first_step(*refs):
    @pl.when(pl.program_id(0) == 0)
    def _():
        for ref in refs:
            ref[...] = jnp.zeros(ref.shape, ref.dtype)


class _Lookahead:
    def __init__(self, xb_ref, w_ref, pn_ref, cur_ref, piece_cols):
        cur_ref[...] = pn_ref[...]
        self._refs = (xb_ref, w_ref, pn_ref)
        width = w_ref.shape[1]
        self._todo = [(c0, min(c0 + piece_cols, width)) for c0 in range(0, width, piece_cols)]

    def issue(self, n=1):
        xb_ref, w_ref, pn_ref = self._refs
        for _ in range(min(n, len(self._todo))):
            c0, c1 = self._todo.pop(0)
            pn_ref[:, c0:c1] = _dot(xb_ref[...], w_ref[:, c0:c1])

    def finish(self):
        self.issue(len(self._todo))


def _mixer_call(body, xb, w, params, *, layer, n_tiles, ts, out_w, scratch, name):
    D = xb.shape[1]
    T = xb.shape[0]
    return pl.pallas_call(
        body,
        out_shape=jax.ShapeDtypeStruct((T, out_w), MXU_DTYPE),
        grid=(n_tiles + 1,),
        in_specs=[
            pl.BlockSpec((ts, D), lambda s: (jnp.minimum(s, n_tiles - 1), 0)),
            _layer_spec(w, layer),
        ] + [_layer_spec(p, layer) for p in params],
        out_specs=pl.BlockSpec((ts, out_w), lambda s: (jnp.maximum(s - 1, 0), 0)),
        scratch_shapes=[pltpu.VMEM((ts, w.shape[-1]), F32),
                        pltpu.VMEM((ts, w.shape[-1]), F32)
                        ] + scratch,
        compiler_params=pltpu.CompilerParams(
            dimension_semantics=("arbitrary",),
            vmem_limit_bytes=V7X_VMEM_LIMIT_BYTES),
        name=name,
    )(xb, w, *params)


def _layer_spec(a, layer):
    return pl.BlockSpec((None,) + a.shape[1:], lambda *_: (layer,) + (0,) * (a.ndim - 1))


def _rglru_kernel(nt, xb_ref, w_ref, cw_ref, cb_ref, wr_ref, br_ref, wi_ref, bi_ref, lam_ref,
                  o_ref, pn_ref, cur_ref, hist_ref, a_ref, u_ref, carry_ref):
    ts = xb_ref.shape[0]
    W = cw_ref.shape[1]
    first = _lookahead_tile(nt)
    _zero_on_first_step(pn_ref, hist_ref, carry_ref)
    look = _Lookahead(xb_ref, w_ref, pn_ref, cur_ref, w_ref.shape[1])
    look.finish()
    xa = jnp.concatenate(_causal_conv(hist_ref, cur_ref, 0, W, cw_ref, first), axis=-1) + cb_ref[...]
    xab = _mx(xa)
    r = _sigmoid(_dot(xab, wr_ref[...]) + br_ref[...])
    i = _sigmoid(_dot(xab, wi_ref[...]) + bi_ref[...])
    log_a = (-A_C) * r * _softplus(-lam_ref[...])
    a = jnp.exp(log_a)
    a_ref[...] = a
    u_ref[...] = jnp.sqrt(1.0 - a * a) * (i * xa)
    rows = _row_iota((V7X_SUBLANES, W))
    h0 = jnp.where(first, 0.0, carry_ref[...])

    def scan8(r0, h_prev):
        ca = a_ref[pl.ds(r0, V7X_SUBLANES), :]
        cb = u_ref[pl.ds(r0, V7X_SUBLANES), :]
        d = 1
        while d < V7X_SUBLANES:
            keep = rows >= d
            cb = jnp.where(keep, ca * pltpu.roll(cb, d, 0) + cb, cb)
            ca = jnp.where(keep, ca * pltpu.roll(ca, d, 0), ca)
            d *= 2
        h = ca * h_prev + cb
        return h, jnp.broadcast_to(h[V7X_SUBLANES - 1:V7X_SUBLANES, :], (V7X_SUBLANES, W))

    def body(blk, h_prev):
        r0 = pl.multiple_of(blk * OUT_ROWS, OUT_ROWS)
        hs = []
        for j in range(OUT_ROWS // V7X_SUBLANES):
            h, h_prev = scan8(r0 + j * V7X_SUBLANES, h_prev)
            hs.append(h)
        gate = _gelu_tanh(cur_ref[pl.ds(r0, OUT_ROWS), W:2 * W])
        o_ref[pl.ds(r0, OUT_ROWS), :] = (jnp.concatenate(hs, axis=0) * gate).astype(o_ref.dtype)
        return h_prev

    carry_ref[...] = lax.fori_loop(0, ts // OUT_ROWS, body, h0, unroll=2)


def _rglru(xb, w, cw, cb, wr, br, wi, bi, lam, *, layer, n_tiles, nt, ts):
    W = cw.shape[-1]
    params = (cw, cb, wr, br, wi, bi, lam)
    scratch = [
        pltpu.VMEM((CONV_PAD, W), F32),
        pltpu.VMEM((ts, W), F32),
        pltpu.VMEM((ts, W), F32),
        pltpu.VMEM((V7X_SUBLANES, W), F32),
    ]
    return _mixer_call(functools.partial(_rglru_kernel, nt), xb, w, params, layer=layer,
                       n_tiles=n_tiles, ts=ts, out_w=W, scratch=scratch, name="rglru")


def _gdn_kernel(nt, xb_ref, w_ref, cw_ref, alog_ref, dtb_ref, nw_ref,
                o_ref, pn_ref, cur_ref, hist_ref, state_ref):
    ts = xb_ref.shape[0]
    C = GDN_CHUNK
    n_chunks = ts // C
    dk = V7X_LANES
    hw = B_HEADS * dk
    wq = cw_ref.shape[1]
    first = _lookahead_tile(nt)
    _zero_on_first_step(pn_ref, hist_ref, state_ref)
    look = _Lookahead(xb_ref, w_ref, pn_ref, cur_ref, w_ref.shape[1])
    look.finish()
    sm = cur_ref[:, wq + hw:wq + hw + SMALL_W]
    qkv = [_silu(s) for s in _causal_conv(hist_ref, cur_ref, 0, wq, cw_ref, first)]
    beta_all = _sigmoid(sm)
    g_all = -jnp.exp(alog_ref[...]) * _softplus(sm + dtb_ref[...])
    gcum_all = _chunk_cumsum(g_all, C)
    gcum_t = gcum_all.T

    ri = _row_iota((C, C))
    ci = _col_iota((C, C))
    lower = ci <= ri
    strict = ci < ri
    eye = (ci == ri).astype(F32)
    scale = np.float32(dk ** -0.5)
    base_mask = (ri // INV_BASE) == (ci // INV_BASE)
    level_masks = []
    n = INV_BASE
    while n < C:
        lm = jnp.logical_and((ri // (2 * n)) == (ci // (2 * n)), (ri // n) != (ci // n))
        level_masks.append(lm.astype(F32).astype(MXU_DTYPE))
        n *= 2

    chains = [(c, h) for c in range(n_chunks) for h in range(B_HEADS)]
    st = {}
    for c, h in chains:
        rs = slice(c * C, (c + 1) * C)
        q = qkv[h][rs]
        k = qkv[B_HEADS + h][rs]
        v = qkv[2 * B_HEADS + h][rs]
        q = q * lax.rsqrt(jnp.sum(q * q, -1, keepdims=True) + L2_EPS) * scale
        k = k * lax.rsqrt(jnp.sum(k * k, -1, keepdims=True) + L2_EPS)
        beta = beta_all[rs, h:h + 1]
        gcol = gcum_all[rs, B_HEADS + h:B_HEADS + h + 1]
        grow = gcum_t[B_HEADS + h:B_HEADS + h + 1, rs]
        decay = jnp.where(lower, jnp.exp(jnp.where(lower, gcol - grow, 0.0)), 0.0)
        qb, kb = _mx(q), _mx(k)
        kk = _dot_nt(kb, kb)
        attn_b = _mx(_dot_nt(qb, kb) * decay)
        a_mat = jnp.where(strict, beta * kk * decay, 0.0)
        a_base = jnp.where(base_mask, a_mat, 0.0)
        st[c, h] = dict(q=q, k=k, v=v, beta=beta, gcol=gcol, attn_b=attn_b, a_b=_mx(a_mat),
                        p=eye - a_base, x=a_base)
    span = 2
    while span < INV_BASE:
        for key in chains:
            d = st[key]
            xb = _mx(d["x"])
            d["x"] = _dot(xb, xb)
        for key in chains:
            d = st[key]
            d["p"] = d["p"] + _dot(_mx(d["p"]), _mx(d["x"]))
        span *= 2
    for lm in level_masks:
        for key in chains:
            d = st[key]
            d["pb"] = _mx(d["p"])
            d["m"] = _dot(d["a_b"] * lm, d["pb"])
        for key in chains:
            d = st[key]
            d["p"] = d["p"] - _dot(d["pb"], _mx(d["m"]))
    for key in chains:
        d = st[key]
        e_g = jnp.exp(d["gcol"])
        rhs = jnp.concatenate([d["v"] * d["beta"], d["k"] * (d["beta"] * e_g)], axis=-1)
        sol = _dot(_mx(d["p"]), _mx(rhs))
        g_last = d["gcol"][C - 1:C, :]
        d.update(u=sol[:, :dk], wb=_mx(sol[:, dk:]), qgb=_mx(d["q"] * e_g),
                 kdb=_mx(d["k"] * jnp.exp(g_last - d["gcol"])), dl=jnp.exp(g_last))
    states = [jnp.where(first, 0.0, state_ref[h]) for h in range(B_HEADS)]
    for c in range(n_chunks):
        rs = slice(c * C, (c + 1) * C)
        for h in range(B_HEADS):
            d = st[c, h]
            sb = _mx(states[h])
            v_new = d["u"] - _dot(d["wb"], sb)
            vb = _mx(v_new)
            o = _dot(d["qgb"], sb) + _dot(d["attn_b"], vb)
            states[h] = states[h] * d["dl"] + _dot_tn(d["kdb"], vb)
            z = cur_ref[rs, wq + h * dk:wq + (h + 1) * dk]
            y = (o * lax.rsqrt(jnp.mean(o * o, -1, keepdims=True) + RMS_EPS) * nw_ref[...]
                 * _silu(z))
            o_ref[rs, h * dk:(h + 1) * dk] = y.astype(o_ref.dtype)
    for h in range(B_HEADS):
        state_ref[h] = states[h]


def _gdn(xb, w, cw, alog, dtb, nw, *, layer, n_tiles, nt, ts):
    wq = cw.shape[-1]
    wz = B_HEADS * V7X_LANES
    params = (cw, alog, dtb, nw)
    scratch = [
        pltpu.VMEM((CONV_PAD, wq), F32),
        pltpu.VMEM((B_HEADS, V7X_LANES, V7X_LANES), F32),
    ]
    return _mixer_call(functools.partial(_gdn_kernel, nt), xb, w, params, layer=layer,
                       n_tiles=n_tiles, ts=ts, out_w=wz, scratch=scratch, name="gdn")


def _gla_kernel(nt, xb_ref, w_ref, wg_ref, bg_ref, nw_ref, o_ref, pn_ref, cur_ref, state_ref):
    ts = xb_ref.shape[0]
    C = GLA_CHUNK
    n_chunks = ts // C
    AT = GLA_ATTN_TILE
    kw = wg_ref.shape[1]
    dv = V7X_LANES
    wv = C_HEADS * dv
    pair_k = V7X_LANES
    dk = pair_k // 2
    n_pairs = kw // pair_k
    first = _lookahead_tile(nt)
    _zero_on_first_step(pn_ref, state_ref)
    look = _Lookahead(xb_ref, w_ref, pn_ref, cur_ref, PROJ_PIECE)
    sm = cur_ref[:, 2 * kw + 2 * wv:2 * kw + 2 * wv + SMALL_W]
    q = cur_ref[:, 0:kw] * np.float32(dk ** -0.5)
    k = cur_ref[:, kw:2 * kw]
    vb = _mx(cur_ref[:, 2 * kw:2 * kw + wv])

    zg = _dot(_mx(sm), wg_ref[...]) + bg_ref[...]
    look.issue(GLA_PIECES_UP_FRONT)
    log_a = -_softplus(-zg) * np.float32(1.0 / C_TAU)
    b_cum = _chunk_cumsum(log_a, C)
    q_in = q * jnp.exp(b_cum)
    kinb = _mx(k * jnp.exp(-b_cum))

    ri = _row_iota((AT, AT))
    ci = _col_iota((AT, AT))
    causal = jnp.logical_and(ci <= ri, (ri // C) == (ci // C))
    lane_head = _col_iota((AT, pair_k)) // dk
    bd_mask = (_row_iota((pair_k, 2 * dv)) // dk) == (_col_iota((pair_k, 2 * dv)) // dv)

    o_intra = {}
    for a in range(ts // AT):
        ra = slice(a * AT, (a + 1) * AT)
        for p in range(n_pairs):
            ks = slice(p * pair_k, (p + 1) * pair_k)
            for e in range(2):
                qm = _mx(jnp.where(lane_head == e, q_in[ra, ks], 0.0))
                attn = jnp.where(causal, _dot_nt(qm, kinb[ra, ks]), 0.0)
                hcol = slice((2 * p + e) * dv, (2 * p + e + 1) * dv)
                o_intra[a, 2 * p + e] = _dot(_mx(attn), vb[ra, hcol])
    o_inter = {}
    for p in range(n_pairs):
        ks = slice(p * pair_k, (p + 1) * pair_k)
        b_p = b_cum[:, ks]
        b_last_rows = [b_p[(c + 1) * C - 1:(c + 1) * C, :] for c in range(n_chunks)]
        b_last_blk = jnp.concatenate(
            b_last_rows + [jnp.zeros((V7X_LANES - n_chunks, pair_k), F32)], axis=0)
        d_last_cols = jnp.exp(b_last_blk).T
        upd = []
        for c in range(n_chunks):
            rs = slice(c * C, (c + 1) * C)
            k_dec = k[rs, ks] * jnp.exp(b_last_rows[c] - b_p[rs])
            upd.append(jnp.where(bd_mask, _dot_tn(_mx(k_dec), vb[rs, p * 2 * dv:(p + 1) * 2 * dv]), 0.0))
        state = jnp.where(first, 0.0, state_ref[p])
        qb_p = _mx(q_in[:, ks])
        for c in range(n_chunks):
            rs = slice(c * C, (c + 1) * C)
            o_inter[p, c] = _dot(qb_p[rs], _mx(state))
            state = state * d_last_cols[:, c:c + 1] + upd[c]
        state_ref[p] = state
    look.finish()
    for p in range(n_pairs):
        o_int = jnp.concatenate([o_inter[p, c] for c in range(n_chunks)], axis=0)
        for e in range(2):
            hd = 2 * p + e
            hcol = slice(hd * dv, (hd + 1) * dv)
            o = (jnp.concatenate([o_intra[a, hd] for a in range(ts // AT)], axis=0)
                 + o_int[:, e * dv:(e + 1) * dv])
            r_gate = _silu(cur_ref[:, 2 * kw + wv + hd * dv:2 * kw + wv + (hd + 1) * dv])
            y = (o * lax.rsqrt(jnp.mean(o * o, -1, keepdims=True) + RMS_EPS) * nw_ref[...]
                 * r_gate)
            o_ref[:, hcol] = y.astype(o_ref.dtype)


def _gla(xb, w, wg, bg, nw, *, layer, n_tiles, nt, ts):
    kw = wg.shape[-1]
    wv = C_HEADS * V7X_LANES
    params = (wg, bg, nw)
    scratch = [
        pltpu.VMEM((kw // V7X_LANES, V7X_LANES, 2 * V7X_LANES), F32),
    ]
    return _mixer_call(functools.partial(_gla_kernel, nt), xb, w, params, layer=layer,
                       n_tiles=n_tiles, ts=ts, out_w=wv, scratch=scratch, name="gla")


def _layer_norm(h, g, b):
    mu = jnp.mean(h, -1, keepdims=True)
    hc = h - mu
    var = jnp.mean(hc * hc, -1, keepdims=True)
    return hc * lax.rsqrt(var + LN_EPS) * g + b


def _merge_kernel(alpha, sub, ya_ref, yb_ref, yc_ref, xb_ref, x_ref, wm_ref, wb_ref, gb_ref,
                  wo_ref, g_ref, b_ref, o_ref, ob_ref):
    tm, D = x_ref.shape
    for r0 in range(0, tm, sub):
        rs = slice(r0, r0 + sub)
        xb = xb_ref[rs, :]
        merged = None
        for gi, y_ref in enumerate((ya_ref, yb_ref, yc_ref)):
            cs = slice(gi * D, (gi + 1) * D)
            gate = _sigmoid(_dot(xb, wm_ref[:, cs]) + gb_ref[:, cs])
            term = gate * _dot(y_ref[rs, :], wb_ref[gi])
            merged = term if merged is None else merged + term
        mix = _dot(_mx(merged), wo_ref[...])
        y = _layer_norm(alpha * x_ref[rs, :] + mix, g_ref[...], b_ref[...])
        o_ref[rs, :] = y
        ob_ref[rs, :] = _mx(y)


def _merge_out(ya, yb, yc, xb, x, wm, wb, gb, wo, ln_g, ln_b, *, layer, alpha, tm, sub):
    T, D = x.shape
    W = ya.shape[1]
    row = lambda w: pl.BlockSpec((tm, w), lambda i: (i, 0))
    params = (wm, wb, gb, wo, ln_g, ln_b)
    return pl.pallas_call(
        functools.partial(_merge_kernel, alpha, sub),
        out_shape=(jax.ShapeDtypeStruct((T, D), F32), jax.ShapeDtypeStruct((T, D), MXU_DTYPE)),
        grid=(T // tm,),
        in_specs=[row(W), row(W), row(W), row(D), row(D)] + [_layer_spec(p, layer) for p in params],
        out_specs=(row(D), row(D)),
        compiler_params=pltpu.CompilerParams(
            dimension_semantics=("parallel",),
            vmem_limit_bytes=V7X_VMEM_LIMIT_BYTES),
        name="merge_out",
    )(ya, yb, yc, xb, x, wm, wb, gb, wo, ln_g, ln_b)


def _ffn_kernel(alpha, sub, x_ref, xb_ref, w1_ref, w3_ref, w2_ref, g_ref, b_ref, o_ref, ob_ref):
    tm = x_ref.shape[0]
    for r0 in range(0, tm, sub):
        rs = slice(r0, r0 + sub)
        xb = xb_ref[rs, :]
        h = _silu(_dot(xb, w1_ref[...])) * _dot(xb, w3_ref[...])
        y = _layer_norm(alpha * x_ref[rs, :] + _dot(_mx(h), w2_ref[...]), g_ref[...], b_ref[...])
        o_ref[rs, :] = y
        ob_ref[rs, :] = _mx(y)


def _ffn(x, xb, w1, w3, w2, ln_g, ln_b, *, layer, alpha, tm, sub):
    T, D = x.shape
    row = pl.BlockSpec((tm, D), lambda i: (i, 0))
    return pl.pallas_call(
        functools.partial(_ffn_kernel, alpha, sub),
        out_shape=(jax.ShapeDtypeStruct((T, D), F32), jax.ShapeDtypeStruct((T, D), MXU_DTYPE)),
        grid=(T // tm,),
        in_specs=[row, row] + [_layer_spec(p, layer) for p in (w1, w3, w2, ln_g, ln_b)],
        out_specs=(row, row),
        compiler_params=pltpu.CompilerParams(
            dimension_semantics=("parallel",),
            vmem_limit_bytes=V7X_VMEM_LIMIT_BYTES),
        name="ffn",
    )(x, xb, w1, w3, w2, ln_g, ln_b)


def _tile(n, pref):
    t = min(n, pref)
    while n % t:
        t -= 1
    return t


def _pad_lanes(v, offset, width):
    return jnp.zeros((v.shape[0], 1, width), F32).at[:, 0, offset:offset + v.shape[1]].set(v)


def kernel(x, w_in, a_conv_w, a_conv_b, a_w_r, a_b_r, a_w_i, a_b_i, a_lambda, b_conv_w, b_a_log,
           b_dt_bias, b_norm_w, c_w_g2, c_b_g2, c_norm_w, gate_b, w_branch, w_out, ln1_g, ln1_b,
           ffn_w1, ffn_w3, ffn_w2, ln2_g, ln2_b):
    B, S, D = x.shape
    depth = w_in.shape[0]
    T = B * S
    a_width = a_conv_w.shape[2]
    qkv_w = b_conv_w.shape[2]
    b_width = B_HEADS * V7X_LANES
    c_kw = c_w_g2.shape[2]
    c_rank = c_w_g2.shape[1]
    c_width = C_HEADS * V7X_LANES
    alpha = float((2.0 * depth) ** 0.25)

    splits = (a_width, a_width, qkv_w, b_width, B_HEADS, B_HEADS, c_kw, c_kw, c_width, c_rank,
              c_width, N_BRANCH * D)
    offs = np.concatenate([[0], np.cumsum(splits)])
    (s_ax, s_ag, s_qkv, s_z, s_beta, s_alpha, s_cq, s_ck, s_cv, s_cg, s_cr, s_merge) = [
        slice(int(offs[i]), int(offs[i + 1])) for i in range(len(splits))]
    n_small = 2 * B_HEADS + c_rank
    w_in_b = _mx(w_in)
    w_small = jnp.concatenate(
        [w_in_b[:, :, s_beta], w_in_b[:, :, s_alpha], w_in_b[:, :, s_cg],
         jnp.zeros((depth, D, SMALL_W - n_small), MXU_DTYPE)], axis=-1)

    def cols(*groups):
        return jnp.concatenate([g if not isinstance(g, slice) else w_in_b[:, :, g] for g in groups],
                               axis=-1)

    w_a = cols(s_ax, s_ag)
    w_b = cols(s_qkv, s_z, w_small)
    w_c = cols(s_cq, s_ck, s_cv, s_cr, w_small)
    w_m = cols(s_merge)

    eye_blocks = jnp.eye(A_BLOCKS, dtype=F32)

    def block_diag(w):
        return jnp.einsum("lhij,hg->lhigj", w, eye_blocks).reshape(depth, a_width, a_width)

    wr_bd = block_diag(a_w_r).astype(MXU_DTYPE)
    wi_bd = block_diag(a_w_i).astype(MXU_DTYPE)
    wg_pad = jnp.zeros((depth, SMALL_W, c_kw), F32).at[:, 2 * B_HEADS:n_small, :].set(c_w_g2)
    wg_pad = wg_pad.astype(MXU_DTYPE)
    wb = w_branch.astype(MXU_DTYPE)
    wo = w_out.astype(MXU_DTYPE)
    w1 = ffn_w1.astype(MXU_DTYPE)
    w3 = ffn_w3.astype(MXU_DTYPE)
    w2 = ffn_w2.astype(MXU_DTYPE)

    ts = _tile(S, 512)
    nt = S // ts
    n_tiles = B * nt
    assert ts % GDN_CHUNK == 0 and ts % GLA_ATTN_TILE == 0 and GLA_ATTN_TILE % GLA_CHUNK == 0
    tm_merge = _tile(T, 1024)
    sub_merge = _tile(tm_merge, 256)
    tm_ffn = _tile(T, 1024)
    sub_ffn = _tile(tm_ffn, 256)

    row = lambda v: v[:, None, :]
    alog_pad = _pad_lanes(b_a_log, B_HEADS, SMALL_W)
    dtb_pad = _pad_lanes(b_dt_bias, B_HEADS, SMALL_W)
    mix = dict(n_tiles=n_tiles, nt=nt, ts=ts)

    h = x.reshape(T, D)
    hb = _mx(h)
    for l in range(depth):
        ya = _rglru(hb, w_a, a_conv_w, row(a_conv_b), wr_bd, row(a_b_r), wi_bd, row(a_b_i),
                    row(a_lambda), layer=l, **mix)
        yb = _gdn(hb, w_b, b_conv_w, alog_pad, dtb_pad, row(b_norm_w), layer=l, **mix)
        yc = _gla(hb, w_c, wg_pad, row(c_b_g2), row(c_norm_w), layer=l, **mix)
        h, hb = _merge_out(ya, yb, yc, hb, h, w_m, wb, row(gate_b), wo, row(ln1_g), row(ln1_b),
                           layer=l, alpha=alpha, tm=tm_merge, sub=sub_merge)
        h, hb = _ffn(h, hb, w1, w3, w2, row(ln2_g), row(ln2_b),
                     layer=l, alpha=alpha, tm=tm_ffn, sub=sub_ffn)
    return h.reshape(B, S, D)
```

```python
import functools

import jax
import jax.numpy as jnp
import numpy as np
from jax import lax
from jax.experimental import pallas as pl
from jax.experimental.pallas import tpu as pltpu

F32 = jnp.float32
MXU_DTYPE = jnp.bfloat16

V7X_LANES = 128
V7X_SUBLANES = 8
V7X_VMEM_LIMIT_BYTES = 56 * 1024 * 1024

N_BRANCH = 3
A_BLOCKS = 8
A_C = 8.0
CONV_K = 4
B_HEADS = 4
C_HEADS = 4
C_TAU = 16.0
GLA_CHUNK = 64
GLA_ATTN_TILE = 256
GDN_CHUNK = 128
INV_BASE = 8
PROJ_PIECE = 256
GLA_PIECES_UP_FRONT = 3
LN_EPS = 1e-5
RMS_EPS = 1e-6
L2_EPS = 1e-6
SMALL_W = V7X_LANES
CONV_PAD = V7X_SUBLANES
OUT_ROWS = 2 * V7X_SUBLANES


def _mx(x):
    return x.astype(MXU_DTYPE)


def _dot(a, b):
    return jnp.dot(a, b, preferred_element_type=F32)


def _dot_nt(a, b):
    return lax.dot_general(a, b, (((1,), (1,)), ((), ())), preferred_element_type=F32)


def _dot_tn(a, b):
    return lax.dot_general(a, b, (((0,), (0,)), ((), ())), preferred_element_type=F32)


def _sigmoid(x):
    return 1.0 / (1.0 + jnp.exp(-x))


def _silu(x):
    return x * _sigmoid(x)


def _softplus(x):
    return jnp.maximum(x, 0.0) + jnp.log1p(jnp.exp(-jnp.abs(x)))


def _gelu_tanh(x):
    c = np.float32(np.sqrt(2.0 / np.pi))
    return 0.5 * x * (1.0 + jnp.tanh(c * (x + 0.044715 * (x * x * x))))


def _row_iota(shape):
    return lax.broadcasted_iota(jnp.int32, shape, 0)


def _col_iota(shape):
    return lax.broadcasted_iota(jnp.int32, shape, 1)


def _chunk_cumsum(x, chunk):
    pos = _row_iota(x.shape) % chunk
    d = 1
    while d < chunk:
        x = x + jnp.where(pos >= d, pltpu.roll(x, d, 0), 0.0)
        d *= 2
    return x


def _causal_conv(hist_ref, x_ref, col0, width, w_ref, first_tile):
    ts = x_ref.shape[0]
    strips = []
    for c0 in range(0, width, V7X_LANES):
        cs = slice(c0, c0 + V7X_LANES)
        x = x_ref[:, col0 + c0:col0 + c0 + V7X_LANES]
        hist = jnp.where(first_tile, 0.0, hist_ref[:, cs])
        ext = jnp.concatenate([hist, x], axis=0)
        acc = x * w_ref[CONV_K - 1:CONV_K, cs]
        for back in range(1, CONV_K):
            shifted = pltpu.roll(ext, back, 0)[CONV_PAD:CONV_PAD + ts, :]
            acc = acc + shifted * w_ref[CONV_K - 1 - back:CONV_K - back, cs]
        hist_ref[:, cs] = x[ts - CONV_PAD:ts, :]
        strips.append(acc)
    return strips


def _lookahead_tile(nt):
    s = pl.program_id(0)
    tile = jnp.maximum(s - 1, 0)
    return lax.rem(tile, nt) == 0


def _zero_on_first_step(*refs):
    @pl.when(pl.program_id(0) == 0)
    def _():
        for ref in refs:
            ref[...] = jnp.zeros(ref.shape, ref.dtype)


class _Lookahead:
    def __init__(self, xb_ref, w_ref, pn_ref, cur_ref, piece_cols):
        cur_ref[...] = pn_ref[...]
        self._refs = (xb_ref, w_ref, pn_ref)
        width = w_ref.shape[1]
        self._todo = [(c0, min(c0 + piece_cols, width)) for c0 in range(0, width, piece_cols)]

    def issue(self, n=1):
        xb_ref, w_ref, pn_ref = self._refs
        for _ in range(min(n, len(self._todo))):
            c0, c1 = self._todo.pop(0)
            pn_ref[:, c0:c1] = _dot(xb_ref[...], w_ref[:, c0:c1])

    def finish(self):
        self.issue(len(self._todo))


def _mixer_call(body, xb, w, params, *, layer, n_tiles, ts, out_w, scratch, name):
    D = xb.shape[1]
    T = xb.shape[0]
    return pl.pallas_call(
        body,
        out_shape=jax.ShapeDtypeStruct((T, out_w), MXU_DTYPE),
        grid=(n_tiles + 1,),
        in_specs=[
            pl.BlockSpec((ts, D), lambda s: (jnp.minimum(s, n_tiles - 1), 0)),
            _layer_spec(w, layer),
        ] + [_layer_spec(p, layer) for p in params],
        out_specs=pl.BlockSpec((ts, out_w), lambda s: (jnp.maximum(s - 1, 0), 0)),
        scratch_shapes=[pltpu.VMEM((ts, w.shape[-1]), F32),
                        pltpu.VMEM((ts, w.shape[-1]), F32)
                        ] + scratch,
        compiler_params=pltpu.CompilerParams(
            dimension_semantics=("arbitrary",),
            vmem_limit_bytes=V7X_VMEM_LIMIT_BYTES),
        name=name,
    )(xb, w, *params)


def _layer_spec(a, layer):
    return pl.BlockSpec((None,) + a.shape[1:], lambda *_: (layer,) + (0,) * (a.ndim - 1))


def _rglru_kernel(nt, xb_ref, w_ref, cw_ref, cb_ref, wr_ref, br_ref, wi_ref, bi_ref, lam_ref,
                  o_ref, pn_ref, cur_ref, hist_ref, a_ref, u_ref, carry_ref):
    ts = xb_ref.shape[0]
    W = cw_ref.shape[1]
    first = _lookahead_tile(nt)
    _zero_on_first_step(pn_ref, hist_ref, carry_ref)
    look = _Lookahead(xb_ref, w_ref, pn_ref, cur_ref, w_ref.shape[1])
    look.finish()
    xa = jnp.concatenate(_causal_conv(hist_ref, cur_ref, 0, W, cw_ref, first), axis=-1) + cb_ref[...]
    xab = _mx(xa)
    r = _sigmoid(_dot(xab, wr_ref[...]) + br_ref[...])
    i = _sigmoid(_dot(xab, wi_ref[...]) + bi_ref[...])
    log_a = (-A_C) * r * _softplus(-lam_ref[...])
    a = jnp.exp(log_a)
    a_ref[...] = a
    u_ref[...] = jnp.sqrt(1.0 - a * a) * (i * xa)
    rows = _row_iota((V7X_SUBLANES, W))
    h0 = jnp.where(first, 0.0, carry_ref[...])

    def scan8(r0, h_prev):
        ca = a_ref[pl.ds(r0, V7X_SUBLANES), :]
        cb = u_ref[pl.ds(r0, V7X_SUBLANES), :]
        d = 1
        while d < V7X_SUBLANES:
            keep = rows >= d
            cb = jnp.where(keep, ca * pltpu.roll(cb, d, 0) + cb, cb)
            ca = jnp.where(keep, ca * pltpu.roll(ca, d, 0), ca)
            d *= 2
        h = ca * h_prev + cb
        return h, jnp.broadcast_to(h[V7X_SUBLANES - 1:V7X_SUBLANES, :], (V7X_SUBLANES, W))

    def body(blk, h_prev):
        r0 = pl.multiple_of(blk * OUT_ROWS, OUT_ROWS)
        hs = []
        for j in range(OUT_ROWS // V7X_SUBLANES):
            h, h_prev = scan8(r0 + j * V7X_SUBLANES, h_prev)
            hs.append(h)
        gate = _gelu_tanh(cur_ref[pl.ds(r0, OUT_ROWS), W:2 * W])
        o_ref[pl.ds(r0, OUT_ROWS), :] = (jnp.concatenate(hs, axis=0) * gate).astype(o_ref.dtype)
        return h_prev

    carry_ref[...] = lax.fori_loop(0, ts // OUT_ROWS, body, h0, unroll=2)


def _rglru(xb, w, cw, cb, wr, br, wi, bi, lam, *, layer, n_tiles, nt, ts):
    W = cw.shape[-1]
    params = (cw, cb, wr, br, wi, bi, lam)
    scratch = [
        pltpu.VMEM((CONV_PAD, W), F32),
        pltpu.VMEM((ts, W), F32),
        pltpu.VMEM((ts, W), F32),
        pltpu.VMEM((V7X_SUBLANES, W), F32),
    ]
    return _mixer_call(functools.partial(_rglru_kernel, nt), xb, w, params, layer=layer,
                       n_tiles=n_tiles, ts=ts, out_w=W, scratch=scratch, name="rglru")


def _gdn_kernel(nt, xb_ref, w_ref, cw_ref, alog_ref, dtb_ref, nw_ref,
                o_ref, pn_ref, cur_ref, hist_ref, state_ref):
    ts = xb_ref.shape[0]
    C = GDN_CHUNK
    n_chunks = ts // C
    dk = V7X_LANES
    hw = B_HEADS * dk
    wq = cw_ref.shape[1]
    first = _lookahead_tile(nt)
    _zero_on_first_step(pn_ref, hist_ref, state_ref)
    look = _Lookahead(xb_ref, w_ref, pn_ref, cur_ref, w_ref.shape[1])
    look.finish()
    sm = cur_ref[:, wq + hw:wq + hw + SMALL_W]
    qkv = [_silu(s) for s in _causal_conv(hist_ref, cur_ref, 0, wq, cw_ref, first)]
    beta_all = _sigmoid(sm)
    g_all = -jnp.exp(alog_ref[...]) * _softplus(sm + dtb_ref[...])
    gcum_all = _chunk_cumsum(g_all, C)
    gcum_t = gcum_all.T

    ri = _row_iota((C, C))
    ci = _col_iota((C, C))
    lower = ci <= ri
    strict = ci < ri
    eye = (ci == ri).astype(F32)
    scale = np.float32(dk ** -0.5)
    base_mask = (ri // INV_BASE) == (ci // INV_BASE)
    level_masks = []
    n = INV_BASE
    while n < C:
        lm = jnp.logical_and((ri // (2 * n)) == (ci // (2 * n)), (ri // n) != (ci // n))
        level_masks.append(lm.astype(F32).astype(MXU_DTYPE))
        n *= 2

    chains = [(c, h) for c in range(n_chunks) for h in range(B_HEADS)]
    st = {}
    for c, h in chains:
        rs = slice(c * C, (c + 1) * C)
        q = qkv[h][rs]
        k = qkv[B_HEADS + h][rs]
        v = qkv[2 * B_HEADS + h][rs]
        q = q * lax.rsqrt(jnp.sum(q * q, -1, keepdims=True) + L2_EPS) * scale
        k = k * lax.rsqrt(jnp.sum(k * k, -1, keepdims=True) + L2_EPS)
        beta = beta_all[rs, h:h + 1]
        gcol = gcum_all[rs, B_HEADS + h:B_HEADS + h + 1]
        grow = gcum_t[B_HEADS + h:B_HEADS + h + 1, rs]
        decay = jnp.where(lower, jnp.exp(jnp.where(lower, gcol - grow, 0.0)), 0.0)
        qb, kb = _mx(q), _mx(k)
        kk = _dot_nt(kb, kb)
        attn_b = _mx(_dot_nt(qb, kb) * decay)
        a_mat = jnp.where(strict, beta * kk * decay, 0.0)
        a_base = jnp.where(base_mask, a_mat, 0.0)
        st[c, h] = dict(q=q, k=k, v=v, beta=beta, gcol=gcol, attn_b=attn_b, a_b=_mx(a_mat),
                        p=eye - a_base, x=a_base)
    span = 2
    while span < INV_BASE:
        for key in chains:
            d = st[key]
            xb = _mx(d["x"])
            d["x"] = _dot(xb, xb)
        for key in chains:
            d = st[key]
            d["p"] = d["p"] + _dot(_mx(d["p"]), _mx(d["x"]))
        span *= 2
    for lm in level_masks:
        for key in chains:
            d = st[key]
            d["pb"] = _mx(d["p"])
            d["m"] = _dot(d["a_b"] * lm, d["pb"])
        for key in chains:
            d = st[key]
            d["p"] = d["p"] - _dot(d["pb"], _mx(d["m"]))
    for key in chains:
        d = st[key]
        e_g = jnp.exp(d["gcol"])
        rhs = jnp.concatenate([d["v"] * d["beta"], d["k"] * (d["beta"] * e_g)], axis=-1)
        d["solb"] = _mx(_dot(_mx(d["p"]), _mx(rhs)))
        g_last = d["gcol"][C - 1:C, :]
        d.update(q_g=d["q"] * e_g, kdb=_mx(d["k"] * jnp.exp(g_last - d["gcol"])),
                 dl=jnp.exp(g_last))
    for key in chains:
        d = st[key]
        kd_sol = _dot_tn(d["kdb"], d["solb"])
        at_sol = _dot(d["attn_b"], d["solb"])
        d.update(o0=at_sol[:, :dk], n_mat=kd_sol[:, :dk],
                 lhs=_mx(jnp.concatenate([d["q_g"] - at_sol[:, dk:], kd_sol[:, dk:]], axis=0)))
    states = [jnp.where(first, 0.0, state_ref[h]) for h in range(B_HEADS)]
    for c in range(n_chunks):
        rs = slice(c * C, (c + 1) * C)
        for h in range(B_HEADS):
            d = st[c, h]
            prod = _dot(d["lhs"], _mx(states[h]))
            o = prod[:C] + d["o0"]
            states[h] = states[h] * d["dl"] - prod[C:] + d["n_mat"]
            z = cur_ref[rs, wq + h * dk:wq + (h + 1) * dk]
            y = (o * lax.rsqrt(jnp.mean(o * o, -1, keepdims=True) + RMS_EPS) * nw_ref[...]
                 * _silu(z))
            o_ref[rs, h * dk:(h + 1) * dk] = y.astype(o_ref.dtype)
    for h in range(B_HEADS):
        state_ref[h] = states[h]


def _gdn(xb, w, cw, alog, dtb, nw, *, layer, n_tiles, nt, ts):
    wq = cw.shape[-1]
    wz = B_HEADS * V7X_LANES
    params = (cw, alog, dtb, nw)
    scratch = [
        pltpu.VMEM((CONV_PAD, wq), F32),
        pltpu.VMEM((B_HEADS, V7X_LANES, V7X_LANES), F32),
    ]
    return _mixer_call(functools.partial(_gdn_kernel, nt), xb, w, params, layer=layer,
                       n_tiles=n_tiles, ts=ts, out_w=wz, scratch=scratch, name="gdn")


def _gla_kernel(nt, xb_ref, w_ref, wg_ref, bg_ref, nw_ref, o_ref, pn_ref, cur_ref, state_ref):
    ts = xb_ref.shape[0]
    C = GLA_CHUNK
    n_chunks = ts // C
    AT = GLA_ATTN_TILE
    kw = wg_ref.shape[1]
    dv = V7X_LANES
    wv = C_HEADS * dv
    pair_k = V7X_LANES
    dk = pair_k // 2
    n_pairs = kw // pair_k
    first = _lookahead_tile(nt)
    _zero_on_first_step(pn_ref, state_ref)
    look = _Lookahead(xb_ref, w_ref, pn_ref, cur_ref, PROJ_PIECE)
    sm = cur_ref[:, 2 * kw + 2 * wv:2 * kw + 2 * wv + SMALL_W]
    q = cur_ref[:, 0:kw] * np.float32(dk ** -0.5)
    k = cur_ref[:, kw:2 * kw]
    vb = _mx(cur_ref[:, 2 * kw:2 * kw + wv])

    zg = _dot(_mx(sm), wg_ref[...]) + bg_ref[...]
    look.issue(GLA_PIECES_UP_FRONT)
    log_a = -_softplus(-zg) * np.float32(1.0 / C_TAU)
    b_cum = _chunk_cumsum(log_a, C)
    q_in = q * jnp.exp(b_cum)
    kinb = _mx(k * jnp.exp(-b_cum))

    ri = _row_iota((AT, AT))
    ci = _col_iota((AT, AT))
    causal = jnp.logical_and(ci <= ri, (ri // C) == (ci // C))
    lane_head = _col_iota((AT, pair_k)) // dk
    bd_mask = (_row_iota((pair_k, 2 * dv)) // dk) == (_col_iota((pair_k, 2 * dv)) // dv)

    o_intra = {}
    for a in range(ts // AT):
        ra = slice(a * AT, (a + 1) * AT)
        for p in range(n_pairs):
            ks = slice(p * pair_k, (p + 1) * pair_k)
            for e in range(2):
                qm = _mx(jnp.where(lane_head == e, q_in[ra, ks], 0.0))
                attn = jnp.where(causal, _dot_nt(qm, kinb[ra, ks]), 0.0)
                hcol = slice((2 * p + e) * dv, (2 * p + e + 1) * dv)
                o_intra[a, 2 * p + e] = _dot(_mx(attn), vb[ra, hcol])
    o_inter = {}
    for p in range(n_pairs):
        ks = slice(p * pair_k, (p + 1) * pair_k)
        b_p = b_cum[:, ks]
        b_last_rows = [b_p[(c + 1) * C - 1:(c + 1) * C, :] for c in range(n_chunks)]
        b_last_blk = jnp.concatenate(
            b_last_rows + [jnp.zeros((V7X_LANES - n_chunks, pair_k), F32)], axis=0)
        d_last_cols = jnp.exp(b_last_blk).T
        upd = []
        for c in range(n_chunks):
            rs = slice(c * C, (c + 1) * C)
            k_dec = k[rs, ks] * jnp.exp(b_last_rows[c] - b_p[rs])
            upd.append(jnp.where(bd_mask, _dot_tn(_mx(k_dec), vb[rs, p * 2 * dv:(p + 1) * 2 * dv]), 0.0))
        state = jnp.where(first, 0.0, state_ref[p])
        qb_p = _mx(q_in[:, ks])
        for c in range(n_chunks):
            rs = slice(c * C, (c + 1) * C)
            o_inter[p, c] = _dot(qb_p[rs], _mx(state))
            state = state * d_last_cols[:, c:c + 1] + upd[c]
        state_ref[p] = state
    look.finish()
    for p in range(n_pairs):
        o_int = jnp.concatenate([o_inter[p, c] for c in range(n_chunks)], axis=0)
        for e in range(2):
            hd = 2 * p + e
            hcol = slice(hd * dv, (hd + 1) * dv)
            o = (jnp.concatenate([o_intra[a, hd] for a in range(ts // AT)], axis=0)
                 + o_int[:, e * dv:(e + 1) * dv])
            r_gate = _silu(cur_ref[:, 2 * kw + wv + hd * dv:2 * kw + wv + (hd + 1) * dv])
            y = (o * lax.rsqrt(jnp.mean(o * o, -1, keepdims=True) + RMS_EPS) * nw_ref[...]
                 * r_gate)
            o_ref[:, hcol] = y.astype(o_ref.dtype)


def _gla(xb, w, wg, bg, nw, *, layer, n_tiles, nt, ts):
    kw = wg.shape[-1]
    wv = C_HEADS * V7X_LANES
    params = (wg, bg, nw)
    scratch = [
        pltpu.VMEM((kw // V7X_LANES, V7X_LANES, 2 * V7X_LANES), F32),
    ]
    return _mixer_call(functools.partial(_gla_kernel, nt), xb, w, params, layer=layer,
                       n_tiles=n_tiles, ts=ts, out_w=wv, scratch=scratch, name="gla")


def _layer_norm(h, g, b):
    mu = jnp.mean(h, -1, keepdims=True)
    hc = h - mu
    var = jnp.mean(hc * hc, -1, keepdims=True)
    return hc * lax.rsqrt(var + LN_EPS) * g + b


def _merge_kernel(alpha, sub, ya_ref, yb_ref, yc_ref, xb_ref, x_ref, wm_ref, wb_ref, gb_ref,
                  wo_ref, g_ref, b_ref, o_ref, ob_ref):
    tm, D = x_ref.shape
    for r0 in range(0, tm, sub):
        rs = slice(r0, r0 + sub)
        xb = xb_ref[rs, :]
        merged = None
        for gi, y_ref in enumerate((ya_ref, yb_ref, yc_ref)):
            cs = slice(gi * D, (gi + 1) * D)
            gate = _sigmoid(_dot(xb, wm_ref[:, cs]) + gb_ref[:, cs])
            term = gate * _dot(y_ref[rs, :], wb_ref[gi])
            merged = term if merged is None else merged + term
        mix = _dot(_mx(merged), wo_ref[...])
        y = _layer_norm(alpha * x_ref[rs, :] + mix, g_ref[...], b_ref[...])
        o_ref[rs, :] = y
        ob_ref[rs, :] = _mx(y)


def _merge_out(ya, yb, yc, xb, x, wm, wb, gb, wo, ln_g, ln_b, *, layer, alpha, tm, sub):
    T, D = x.shape
    W = ya.shape[1]
    row = lambda w: pl.BlockSpec((tm, w), lambda i: (i, 0))
    params = (wm, wb, gb, wo, ln_g, ln_b)
    return pl.pallas_call(
        functools.partial(_merge_kernel, alpha, sub),
        out_shape=(jax.ShapeDtypeStruct((T, D), F32), jax.ShapeDtypeStruct((T, D), MXU_DTYPE)),
        grid=(T // tm,),
        in_specs=[row(W), row(W), row(W), row(D), row(D)] + [_layer_spec(p, layer) for p in params],
        out_specs=(row(D), row(D)),
        compiler_params=pltpu.CompilerParams(
            dimension_semantics=("parallel",),
            vmem_limit_bytes=V7X_VMEM_LIMIT_BYTES),
        name="merge_out",
    )(ya, yb, yc, xb, x, wm, wb, gb, wo, ln_g, ln_b)


def _ffn_kernel(alpha, sub, x_ref, xb_ref, w1_ref, w3_ref, w2_ref, g_ref, b_ref, o_ref, ob_ref):
    tm = x_ref.shape[0]
    for r0 in range(0, tm, sub):
        rs = slice(r0, r0 + sub)
        xb = xb_ref[rs, :]
        h = _silu(_dot(xb, w1_ref[...])) * _dot(xb, w3_ref[...])
        y = _layer_norm(alpha * x_ref[rs, :] + _dot(_mx(h), w2_ref[...]), g_ref[...], b_ref[...])
        o_ref[rs, :] = y
        ob_ref[rs, :] = _mx(y)


def _ffn(x, xb, w1, w3, w2, ln_g, ln_b, *, layer, alpha, tm, sub):
    T, D = x.shape
    row = pl.BlockSpec((tm, D), lambda i: (i, 0))
    return pl.pallas_call(
        functools.partial(_ffn_kernel, alpha, sub),
        out_shape=(jax.ShapeDtypeStruct((T, D), F32), jax.ShapeDtypeStruct((T, D), MXU_DTYPE)),
        grid=(T // tm,),
        in_specs=[row, row] + [_layer_spec(p, layer) for p in (w1, w3, w2, ln_g, ln_b)],
        out_specs=(row, row),
        compiler_params=pltpu.CompilerParams(
            dimension_semantics=("parallel",),
            vmem_limit_bytes=V7X_VMEM_LIMIT_BYTES),
        name="ffn",
    )(x, xb, w1, w3, w2, ln_g, ln_b)


def _tile(n, pref):
    t = min(n, pref)
    while n % t:
        t -= 1
    return t


def _pad_lanes(v, offset, width):
    return jnp.zeros((v.shape[0], 1, width), F32).at[:, 0, offset:offset + v.shape[1]].set(v)


def kernel(x, w_in, a_conv_w, a_conv_b, a_w_r, a_b_r, a_w_i, a_b_i, a_lambda, b_conv_w, b_a_log,
           b_dt_bias, b_norm_w, c_w_g2, c_b_g2, c_norm_w, gate_b, w_branch, w_out, ln1_g, ln1_b,
           ffn_w1, ffn_w3, ffn_w2, ln2_g, ln2_b):
    B, S, D = x.shape
    depth = w_in.shape[0]
    T = B * S
    a_width = a_conv_w.shape[2]
    qkv_w = b_conv_w.shape[2]
    b_width = B_HEADS * V7X_LANES
    c_kw = c_w_g2.shape[2]
    c_rank = c_w_g2.shape[1]
    c_width = C_HEADS * V7X_LANES
    alpha = float((2.0 * depth) ** 0.25)

    splits = (a_width, a_width, qkv_w, b_width, B_HEADS, B_HEADS, c_kw, c_kw, c_width, c_rank,
              c_width, N_BRANCH * D)
    offs = np.concatenate([[0], np.cumsum(splits)])
    (s_ax, s_ag, s_qkv, s_z, s_beta, s_alpha, s_cq, s_ck, s_cv, s_cg, s_cr, s_merge) = [
        slice(int(offs[i]), int(offs[i + 1])) for i in range(len(splits))]
    n_small = 2 * B_HEADS + c_rank
    w_in_b = _mx(w_in)
    w_small = jnp.concatenate(
        [w_in_b[:, :, s_beta], w_in_b[:, :, s_alpha], w_in_b[:, :, s_cg],
         jnp.zeros((depth, D, SMALL_W - n_small), MXU_DTYPE)], axis=-1)

    def cols(*groups):
        return jnp.concatenate([g if not isinstance(g, slice) else w_in_b[:, :, g] for g in groups],
                               axis=-1)

    w_a = cols(s_ax, s_ag)
    w_b = cols(s_qkv, s_z, w_small)
    w_c = cols(s_cq, s_ck, s_cv, s_cr, w_small)
    w_m = cols(s_merge)

    eye_blocks = jnp.eye(A_BLOCKS, dtype=F32)

    def block_diag(w):
        return jnp.einsum("lhij,hg->lhigj", w, eye_blocks).reshape(depth, a_width, a_width)

    wr_bd = block_diag(a_w_r).astype(MXU_DTYPE)
    wi_bd = block_diag(a_w_i).astype(MXU_DTYPE)
    wg_pad = jnp.zeros((depth, SMALL_W, c_kw), F32).at[:, 2 * B_HEADS:n_small, :].set(c_w_g2)
    wg_pad = wg_pad.astype(MXU_DTYPE)
    wb = w_branch.astype(MXU_DTYPE)
    wo = w_out.astype(MXU_DTYPE)
    w1 = ffn_w1.astype(MXU_DTYPE)
    w3 = ffn_w3.astype(MXU_DTYPE)
    w2 = ffn_w2.astype(MXU_DTYPE)

    ts = _tile(S, 512)
    nt = S // ts
    n_tiles = B * nt
    assert ts % GDN_CHUNK == 0 and ts % GLA_ATTN_TILE == 0 and GLA_ATTN_TILE % GLA_CHUNK == 0
    tm_merge = _tile(T, 1024)
    sub_merge = _tile(tm_merge, 256)
    tm_ffn = _tile(T, 1024)
    sub_ffn = _tile(tm_ffn, 256)

    row = lambda v: v[:, None, :]
    alog_pad = _pad_lanes(b_a_log, B_HEADS, SMALL_W)
    dtb_pad = _pad_lanes(b_dt_bias, B_HEADS, SMALL_W)
    mix = dict(n_tiles=n_tiles, nt=nt, ts=ts)

    h = x.reshape(T, D)
    hb = _mx(h)
    for l in range(depth):
        ya = _rglru(hb, w_a, a_conv_w, row(a_conv_b), wr_bd, row(a_b_r), wi_bd, row(a_b_i),
                    row(a_lambda), layer=l, **mix)
        yb = _gdn(hb, w_b, b_conv_w, alog_pad, dtb_pad, row(b_norm_w), layer=l, **mix)
        yc = _gla(hb, w_c, wg_pad, row(c_b_g2), row(c_norm_w), layer=l, **mix)
        h, hb = _merge_out(ya, yb, yc, hb, h, w_m, wb, row(gate_b), wo, row(ln1_g), row(ln1_b),
                           layer=l, alpha=alpha, tm=tm_merge, sub=sub_merge)
        h, hb = _ffn(h, hb, w1, w3, w2, row(ln2_g), row(ln2_b),
                     layer=l, alpha=alpha, tm=tm_ffn, sub=sub_ffn)
    return h.reshape(B, S, D)
```

```python
import functools

import jax
import jax.numpy as jnp
import numpy as np
from jax import lax
from jax.experimental import pallas as pl
from jax.experimental.pallas import tpu as pltpu

F32 = jnp.float32
MXU_DTYPE = jnp.bfloat16

V7X_LANES = 128
V7X_SUBLANES = 8
V7X_VMEM_LIMIT_BYTES = 56 * 1024 * 1024

N_BRANCH = 3
A_BLOCKS = 8
A_C = 8.0
CONV_K = 4
B_HEADS = 4
C_HEADS = 4
C_TAU = 16.0
GLA_CHUNK = 64
GLA_ATTN_TILE = 256
GDN_CHUNK = 128
INV_BASE = 8
PROJ_PIECE = 256
GLA_PIECES_UP_FRONT = 3
LN_EPS = 1e-5
RMS_EPS = 1e-6
L2_EPS = 1e-6
SMALL_W = V7X_LANES
CONV_PAD = V7X_SUBLANES
OUT_ROWS = 2 * V7X_SUBLANES


def _mx(x):
    return x.astype(MXU_DTYPE)


def _dot(a, b):
    return jnp.dot(a, b, preferred_element_type=F32)


def _dot_nt(a, b):
    return lax.dot_general(a, b, (((1,), (1,)), ((), ())), preferred_element_type=F32)


def _dot_tn(a, b):
    return lax.dot_general(a, b, (((0,), (0,)), ((), ())), preferred_element_type=F32)


def _sigmoid(x):
    return 0.5 * jnp.tanh(0.5 * x) + 0.5


def _silu(x):
    hx = 0.5 * x
    return hx * jnp.tanh(hx) + hx


def _softplus(x):
    return jnp.maximum(x, 0.0) + jnp.log1p(jnp.exp(-jnp.abs(x)))


def _gelu_tanh(x):
    c = np.float32(np.sqrt(2.0 / np.pi))
    return 0.5 * x * (1.0 + jnp.tanh(c * (x + 0.044715 * (x * x * x))))


def _row_iota(shape):
    return lax.broadcasted_iota(jnp.int32, shape, 0)


def _col_iota(shape):
    return lax.broadcasted_iota(jnp.int32, shape, 1)


def _chunk_cumsum(x, chunk):
    pos = _row_iota(x.shape) % chunk
    d = 1
    while d < chunk:
        x = x + jnp.where(pos >= d, pltpu.roll(x, d, 0), 0.0)
        d *= 2
    return x


def _causal_conv(hist_ref, x_ref, col0, width, w_ref, first_tile):
    ts = x_ref.shape[0]
    strips = []
    for c0 in range(0, width, V7X_LANES):
        cs = slice(c0, c0 + V7X_LANES)
        x = x_ref[:, col0 + c0:col0 + c0 + V7X_LANES]
        hist = jnp.where(first_tile, 0.0, hist_ref[:, cs])
        ext = jnp.concatenate([hist, x], axis=0)
        acc = x * w_ref[CONV_K - 1:CONV_K, cs]
        for back in range(1, CONV_K):
            shifted = pltpu.roll(ext, back, 0)[CONV_PAD:CONV_PAD + ts, :]
            acc = acc + shifted * w_ref[CONV_K - 1 - back:CONV_K - back, cs]
        hist_ref[:, cs] = x[ts - CONV_PAD:ts, :]
        strips.append(acc)
    return strips


def _lookahead_tile(nt):
    s = pl.program_id(0)
    tile = jnp.maximum(s - 1, 0)
    return lax.rem(tile, nt) == 0


def _zero_on_first_step(*refs):
    @pl.when(pl.program_id(0) == 0)
    def _():
        for ref in refs:
            ref[...] = jnp.zeros(ref.shape, ref.dtype)


class _Lookahead:
    def __init__(self, xb_ref, w_ref, pn_ref, cur_ref, piece_cols):
        cur_ref[...] = pn_ref[...]
        self._refs = (xb_ref, w_ref, pn_ref)
        width = w_ref.shape[1]
        self._todo = [(c0, min(c0 + piece_cols, width)) for c0 in range(0, width, piece_cols)]

    def issue(self, n=1):
        xb_ref, w_ref, pn_ref = self._refs
        for _ in range(min(n, len(self._todo))):
            c0, c1 = self._todo.pop(0)
            pn_ref[:, c0:c1] = _dot(xb_ref[...], w_ref[:, c0:c1])

    def finish(self):
        self.issue(len(self._todo))


def _mixer_call(body, xb, w, params, *, layer, n_tiles, ts, out_w, scratch, name):
    D = xb.shape[1]
    T = xb.shape[0]
    return pl.pallas_call(
        body,
        out_shape=jax.ShapeDtypeStruct((T, out_w), MXU_DTYPE),
        grid=(n_tiles + 1,),
        in_specs=[
            pl.BlockSpec((ts, D), lambda s: (jnp.minimum(s, n_tiles - 1), 0)),
            _layer_spec(w, layer),
        ] + [_layer_spec(p, layer) for p in params],
        out_specs=pl.BlockSpec((ts, out_w), lambda s: (jnp.maximum(s - 1, 0), 0)),
        scratch_shapes=[pltpu.VMEM((ts, w.shape[-1]), F32),
                        pltpu.VMEM((ts, w.shape[-1]), F32)
                        ] + scratch,
        compiler_params=pltpu.CompilerParams(
            dimension_semantics=("arbitrary",),
            vmem_limit_bytes=V7X_VMEM_LIMIT_BYTES),
        name=name,
    )(xb, w, *params)


def _layer_spec(a, layer):
    return pl.BlockSpec((None,) + a.shape[1:], lambda *_: (layer,) + (0,) * (a.ndim - 1))


def _rglru_kernel(nt, xb_ref, w_ref, cw_ref, cb_ref, wr_ref, br_ref, wi_ref, bi_ref, lam_ref,
                  o_ref, pn_ref, cur_ref, hist_ref, a_ref, u_ref, carry_ref):
    ts = xb_ref.shape[0]
    W = cw_ref.shape[1]
    first = _lookahead_tile(nt)
    _zero_on_first_step(pn_ref, hist_ref, carry_ref)
    look = _Lookahead(xb_ref, w_ref, pn_ref, cur_ref, w_ref.shape[1])
    look.finish()
    xa = jnp.concatenate(_causal_conv(hist_ref, cur_ref, 0, W, cw_ref, first), axis=-1) + cb_ref[...]
    xab = _mx(xa)
    r = _sigmoid(_dot(xab, wr_ref[...]) + br_ref[...])
    i = _sigmoid(_dot(xab, wi_ref[...]) + bi_ref[...])
    log_a = (-A_C) * r * _softplus(-lam_ref[...])
    a = jnp.exp(log_a)
    a_ref[...] = a
    u_ref[...] = jnp.sqrt(1.0 - a * a) * (i * xa)
    rows = _row_iota((V7X_SUBLANES, W))
    h0 = jnp.where(first, 0.0, carry_ref[...])

    def scan8(r0, h_prev):
        ca = a_ref[pl.ds(r0, V7X_SUBLANES), :]
        cb = u_ref[pl.ds(r0, V7X_SUBLANES), :]
        d = 1
        while d < V7X_SUBLANES:
            keep = rows >= d
            cb = jnp.where(keep, ca * pltpu.roll(cb, d, 0) + cb, cb)
            ca = jnp.where(keep, ca * pltpu.roll(ca, d, 0), ca)
            d *= 2
        h = ca * h_prev + cb
        return h, jnp.broadcast_to(h[V7X_SUBLANES - 1:V7X_SUBLANES, :], (V7X_SUBLANES, W))

    def body(blk, h_prev):
        r0 = pl.multiple_of(blk * OUT_ROWS, OUT_ROWS)
        hs = []
        for j in range(OUT_ROWS // V7X_SUBLANES):
            h, h_prev = scan8(r0 + j * V7X_SUBLANES, h_prev)
            hs.append(h)
        gate = _gelu_tanh(cur_ref[pl.ds(r0, OUT_ROWS), W:2 * W])
        o_ref[pl.ds(r0, OUT_ROWS), :] = (jnp.concatenate(hs, axis=0) * gate).astype(o_ref.dtype)
        return h_prev

    carry_ref[...] = lax.fori_loop(0, ts // OUT_ROWS, body, h0, unroll=2)


def _rglru(xb, w, cw, cb, wr, br, wi, bi, lam, *, layer, n_tiles, nt, ts):
    W = cw.shape[-1]
    params = (cw, cb, wr, br, wi, bi, lam)
    scratch = [
        pltpu.VMEM((CONV_PAD, W), F32),
        pltpu.VMEM((ts, W), F32),
        pltpu.VMEM((ts, W), F32),
        pltpu.VMEM((V7X_SUBLANES, W), F32),
    ]
    return _mixer_call(functools.partial(_rglru_kernel, nt), xb, w, params, layer=layer,
                       n_tiles=n_tiles, ts=ts, out_w=W, scratch=scratch, name="rglru")


def _gdn_kernel(nt, xb_ref, w_ref, cw_ref, alog_ref, dtb_ref, nw_ref,
                o_ref, pn_ref, cur_ref, hist_ref, state_ref):
    ts = xb_ref.shape[0]
    C = GDN_CHUNK
    n_chunks = ts // C
    dk = V7X_LANES
    hw = B_HEADS * dk
    wq = cw_ref.shape[1]
    first = _lookahead_tile(nt)
    _zero_on_first_step(pn_ref, hist_ref, state_ref)
    look = _Lookahead(xb_ref, w_ref, pn_ref, cur_ref, w_ref.shape[1])
    look.finish()
    sm = cur_ref[:, wq + hw:wq + hw + SMALL_W]
    qkv = [_silu(s) for s in _causal_conv(hist_ref, cur_ref, 0, wq, cw_ref, first)]
    beta_all = _sigmoid(sm)
    g_all = -jnp.exp(alog_ref[...]) * _softplus(sm + dtb_ref[...])
    gcum_all = _chunk_cumsum(g_all, C)
    gcum_t = gcum_all.T

    ri = _row_iota((C, C))
    ci = _col_iota((C, C))
    lower = ci <= ri
    strict = ci < ri
    eye = (ci == ri).astype(F32)
    scale = np.float32(dk ** -0.5)
    base_mask = (ri // INV_BASE) == (ci // INV_BASE)
    level_masks = []
    n = INV_BASE
    while n < C:
        lm = jnp.logical_and((ri // (2 * n)) == (ci // (2 * n)), (ri // n) != (ci // n))
        level_masks.append(lm.astype(F32).astype(MXU_DTYPE))
        n *= 2

    chains = [(c, h) for c in range(n_chunks) for h in range(B_HEADS)]
    st = {}
    for c, h in chains:
        rs = slice(c * C, (c + 1) * C)
        q = qkv[h][rs]
        k = qkv[B_HEADS + h][rs]
        v = qkv[2 * B_HEADS + h][rs]
        q = q * lax.rsqrt(jnp.sum(q * q, -1, keepdims=True) + L2_EPS) * scale
        k = k * lax.rsqrt(jnp.sum(k * k, -1, keepdims=True) + L2_EPS)
        beta = beta_all[rs, h:h + 1]
        gcol = gcum_all[rs, B_HEADS + h:B_HEADS + h + 1]
        grow = gcum_t[B_HEADS + h:B_HEADS + h + 1, rs]
        decay = jnp.where(lower, jnp.exp(jnp.where(lower, gcol - grow, 0.0)), 0.0)
        qb, kb = _mx(q), _mx(k)
        kk = _dot_nt(kb, kb)
        attn_b = _mx(_dot_nt(qb, kb) * decay)
        a_mat = jnp.where(strict, beta * kk * decay, 0.0)
        a_base = jnp.where(base_mask, a_mat, 0.0)
        st[c, h] = dict(q=q, k=k, v=v, beta=beta, gcol=gcol, attn_b=attn_b, a_b=_mx(a_mat),
                        p=eye - a_base, x=a_base)
    span = 2
    while span < INV_BASE:
        for key in chains:
            d = st[key]
            xb = _mx(d["x"])
            d["x"] = _dot(xb, xb)
        for key in chains:
            d = st[key]
            d["p"] = d["p"] + _dot(_mx(d["p"]), _mx(d["x"]))
        span *= 2
    for lm in level_masks:
        for key in chains:
            d = st[key]
            d["pb"] = _mx(d["p"])
            d["m"] = _dot(d["a_b"] * lm, d["pb"])
        for key in chains:
            d = st[key]
            d["p"] = d["p"] - _dot(d["pb"], _mx(d["m"]))
    for key in chains:
        d = st[key]
        e_g = jnp.exp(d["gcol"])
        rhs = jnp.concatenate([d["v"] * d["beta"], d["k"] * (d["beta"] * e_g)], axis=-1)
        d["solb"] = _mx(_dot(_mx(d["p"]), _mx(rhs)))
        g_last = d["gcol"][C - 1:C, :]
        d.update(q_g=d["q"] * e_g, kdb=_mx(d["k"] * jnp.exp(g_last - d["gcol"])),
                 dl=jnp.exp(g_last))
    for key in chains:
        d = st[key]
        kd_sol = _dot_tn(d["kdb"], d["solb"])
        at_sol = _dot(d["attn_b"], d["solb"])
        d.update(o0=at_sol[:, :dk], n_mat=kd_sol[:, :dk],
                 lhs=_mx(jnp.concatenate([d["q_g"] - at_sol[:, dk:], kd_sol[:, dk:]], axis=0)))
    states = [jnp.where(first, 0.0, state_ref[h]) for h in range(B_HEADS)]
    for c in range(n_chunks):
        rs = slice(c * C, (c + 1) * C)
        for h in range(B_HEADS):
            d = st[c, h]
            prod = _dot(d["lhs"], _mx(states[h]))
            o = prod[:C] + d["o0"]
            states[h] = states[h] * d["dl"] - prod[C:] + d["n_mat"]
            z = cur_ref[rs, wq + h * dk:wq + (h + 1) * dk]
            y = (o * lax.rsqrt(jnp.mean(o * o, -1, keepdims=True) + RMS_EPS) * nw_ref[...]
                 * _silu(z))
            o_ref[rs, h * dk:(h + 1) * dk] = y.astype(o_ref.dtype)
    for h in range(B_HEADS):
        state_ref[h] = states[h]


def _gdn(xb, w, cw, alog, dtb, nw, *, layer, n_tiles, nt, ts):
    wq = cw.shape[-1]
    wz = B_HEADS * V7X_LANES
    params = (cw, alog, dtb, nw)
    scratch = [
        pltpu.VMEM((CONV_PAD, wq), F32),
        pltpu.VMEM((B_HEADS, V7X_LANES, V7X_LANES), F32),
    ]
    return _mixer_call(functools.partial(_gdn_kernel, nt), xb, w, params, layer=layer,
                       n_tiles=n_tiles, ts=ts, out_w=wz, scratch=scratch, name="gdn")


def _gla_kernel(nt, xb_ref, w_ref, wg_ref, bg_ref, nw_ref, o_ref, pn_ref, cur_ref, state_ref):
    ts = xb_ref.shape[0]
    C = GLA_CHUNK
    n_chunks = ts // C
    AT = GLA_ATTN_TILE
    kw = wg_ref.shape[1]
    dv = V7X_LANES
    wv = C_HEADS * dv
    pair_k = V7X_LANES
    dk = pair_k // 2
    n_pairs = kw // pair_k
    first = _lookahead_tile(nt)
    _zero_on_first_step(pn_ref, state_ref)
    look = _Lookahead(xb_ref, w_ref, pn_ref, cur_ref, PROJ_PIECE)
    sm = cur_ref[:, 2 * kw + 2 * wv:2 * kw + 2 * wv + SMALL_W]
    q = cur_ref[:, 0:kw] * np.float32(dk ** -0.5)
    k = cur_ref[:, kw:2 * kw]
    vb = _mx(cur_ref[:, 2 * kw:2 * kw + wv])

    zg = _dot(_mx(sm), wg_ref[...]) + bg_ref[...]
    look.issue(GLA_PIECES_UP_FRONT)
    log_a = -_softplus(-zg) * np.float32(1.0 / C_TAU)
    b_cum = _chunk_cumsum(log_a, C)
    q_in = q * jnp.exp(b_cum)
    kinb = _mx(k * jnp.exp(-b_cum))

    ri = _row_iota((AT, AT))
    ci = _col_iota((AT, AT))
    causal = jnp.logical_and(ci <= ri, (ri // C) == (ci // C))
    lane_head = _col_iota((AT, pair_k)) // dk
    bd_mask = (_row_iota((pair_k, 2 * dv)) // dk) == (_col_iota((pair_k, 2 * dv)) // dv)

    o_intra = {}
    for a in range(ts // AT):
        ra = slice(a * AT, (a + 1) * AT)
        for p in range(n_pairs):
            ks = slice(p * pair_k, (p + 1) * pair_k)
            for e in range(2):
                qm = _mx(jnp.where(lane_head == e, q_in[ra, ks], 0.0))
                attn = jnp.where(causal, _dot_nt(qm, kinb[ra, ks]), 0.0)
                hcol = slice((2 * p + e) * dv, (2 * p + e + 1) * dv)
                o_intra[a, 2 * p + e] = _dot(_mx(attn), vb[ra, hcol])
    o_inter = {}
    for p in range(n_pairs):
        ks = slice(p * pair_k, (p + 1) * pair_k)
        b_p = b_cum[:, ks]
        b_last_rows = [b_p[(c + 1) * C - 1:(c + 1) * C, :] for c in range(n_chunks)]
        b_last_blk = jnp.concatenate(
            b_last_rows + [jnp.zeros((V7X_LANES - n_chunks, pair_k), F32)], axis=0)
        d_last_cols = jnp.exp(b_last_blk).T
        upd = []
        for c in range(n_chunks):
            rs = slice(c * C, (c + 1) * C)
            k_dec = k[rs, ks] * jnp.exp(b_last_rows[c] - b_p[rs])
            upd.append(jnp.where(bd_mask, _dot_tn(_mx(k_dec), vb[rs, p * 2 * dv:(p + 1) * 2 * dv]), 0.0))
        state = jnp.where(first, 0.0, state_ref[p])
        qb_p = _mx(q_in[:, ks])
        for c in range(n_chunks):
            rs = slice(c * C, (c + 1) * C)
            o_inter[p, c] = _dot(qb_p[rs], _mx(state))
            state = state * d_last_cols[:, c:c + 1] + upd[c]
        state_ref[p] = state
    look.finish()
    for p in range(n_pairs):
        o_int = jnp.concatenate([o_inter[p, c] for c in range(n_chunks)], axis=0)
        for e in range(2):
            hd = 2 * p + e
            hcol = slice(hd * dv, (hd + 1) * dv)
            o = (jnp.concatenate([o_intra[a, hd] for a in range(ts // AT)], axis=0)
                 + o_int[:, e * dv:(e + 1) * dv])
            r_gate = _silu(cur_ref[:, 2 * kw + wv + hd * dv:2 * kw + wv + (hd + 1) * dv])
            y = (o * lax.rsqrt(jnp.mean(o * o, -1, keepdims=True) + RMS_EPS) * nw_ref[...]
                 * r_gate)
            o_ref[:, hcol] = y.astype(o_ref.dtype)


def _gla(xb, w, wg, bg, nw, *, layer, n_tiles, nt, ts):
    kw = wg.shape[-1]
    wv = C_HEADS * V7X_LANES
    params = (wg, bg, nw)
    scratch = [
        pltpu.VMEM((kw // V7X_LANES, V7X_LANES, 2 * V7X_LANES), F32),
    ]
    return _mixer_call(functools.partial(_gla_kernel, nt), xb, w, params, layer=layer,
                       n_tiles=n_tiles, ts=ts, out_w=wv, scratch=scratch, name="gla")


def _layer_norm(h, g, b):
    mu = jnp.mean(h, -1, keepdims=True)
    hc = h - mu
    var = jnp.mean(hc * hc, -1, keepdims=True)
    return hc * lax.rsqrt(var + LN_EPS) * g + b


def _merge_kernel(alpha, sub, ya_ref, yb_ref, yc_ref, xb_ref, x_ref, wm_ref, wb_ref, gb_ref,
                  wo_ref, g_ref, b_ref, o_ref, ob_ref):
    tm, D = x_ref.shape
    for r0 in range(0, tm, sub):
        rs = slice(r0, r0 + sub)
        xb = xb_ref[rs, :]
        merged = None
        for gi, y_ref in enumerate((ya_ref, yb_ref, yc_ref)):
            cs = slice(gi * D, (gi + 1) * D)
            gate = _sigmoid(_dot(xb, wm_ref[:, cs]) + gb_ref[:, cs])
            term = gate * _dot(y_ref[rs, :], wb_ref[gi])
            merged = term if merged is None else merged + term
        mix = _dot(_mx(merged), wo_ref[...])
        y = _layer_norm(alpha * x_ref[rs, :] + mix, g_ref[...], b_ref[...])
        o_ref[rs, :] = y
        ob_ref[rs, :] = _mx(y)


def _merge_out(ya, yb, yc, xb, x, wm, wb, gb, wo, ln_g, ln_b, *, layer, alpha, tm, sub):
    T, D = x.shape
    W = ya.shape[1]
    row = lambda w: pl.BlockSpec((tm, w), lambda i: (i, 0))
    params = (wm, wb, gb, wo, ln_g, ln_b)
    return pl.pallas_call(
        functools.partial(_merge_kernel, alpha, sub),
        out_shape=(jax.ShapeDtypeStruct((T, D), F32), jax.ShapeDtypeStruct((T, D), MXU_DTYPE)),
        grid=(T // tm,),
        in_specs=[row(W), row(W), row(W), row(D), row(D)] + [_layer_spec(p, layer) for p in params],
        out_specs=(row(D), row(D)),
        compiler_params=pltpu.CompilerParams(
            dimension_semantics=("parallel",),
            vmem_limit_bytes=V7X_VMEM_LIMIT_BYTES),
        name="merge_out",
    )(ya, yb, yc, xb, x, wm, wb, gb, wo, ln_g, ln_b)


def _ffn_kernel(alpha, sub, x_ref, xb_ref, w1_ref, w3_ref, w2_ref, g_ref, b_ref, o_ref, ob_ref):
    tm = x_ref.shape[0]
    for r0 in range(0, tm, sub):
        rs = slice(r0, r0 + sub)
        xb = xb_ref[rs, :]
        h = _silu(_dot(xb, w1_ref[...])) * _dot(xb, w3_ref[...])
        y = _layer_norm(alpha * x_ref[rs, :] + _dot(_mx(h), w2_ref[...]), g_ref[...], b_ref[...])
        o_ref[rs, :] = y
        ob_ref[rs, :] = _mx(y)


def _ffn(x, xb, w1, w3, w2, ln_g, ln_b, *, layer, alpha, tm, sub):
    T, D = x.shape
    row = pl.BlockSpec((tm, D), lambda i: (i, 0))
    return pl.pallas_call(
        functools.partial(_ffn_kernel, alpha, sub),
        out_shape=(jax.ShapeDtypeStruct((T, D), F32), jax.ShapeDtypeStruct((T, D), MXU_DTYPE)),
        grid=(T // tm,),
        in_specs=[row, row] + [_layer_spec(p, layer) for p in (w1, w3, w2, ln_g, ln_b)],
        out_specs=(row, row),
        compiler_params=pltpu.CompilerParams(
            dimension_semantics=("parallel",),
            vmem_limit_bytes=V7X_VMEM_LIMIT_BYTES),
        name="ffn",
    )(x, xb, w1, w3, w2, ln_g, ln_b)


def _tile(n, pref):
    t = min(n, pref)
    while n % t:
        t -= 1
    return t


def _pad_lanes(v, offset, width):
    return jnp.zeros((v.shape[0], 1, width), F32).at[:, 0, offset:offset + v.shape[1]].set(v)


def kernel(x, w_in, a_conv_w, a_conv_b, a_w_r, a_b_r, a_w_i, a_b_i, a_lambda, b_conv_w, b_a_log,
           b_dt_bias, b_norm_w, c_w_g2, c_b_g2, c_norm_w, gate_b, w_branch, w_out, ln1_g, ln1_b,
           ffn_w1, ffn_w3, ffn_w2, ln2_g, ln2_b):
    B, S, D = x.shape
    depth = w_in.shape[0]
    T = B * S
    a_width = a_conv_w.shape[2]
    qkv_w = b_conv_w.shape[2]
    b_width = B_HEADS * V7X_LANES
    c_kw = c_w_g2.shape[2]
    c_rank = c_w_g2.shape[1]
    c_width = C_HEADS * V7X_LANES
    alpha = float((2.0 * depth) ** 0.25)

    splits = (a_width, a_width, qkv_w, b_width, B_HEADS, B_HEADS, c_kw, c_kw, c_width, c_rank,
              c_width, N_BRANCH * D)
    offs = np.concatenate([[0], np.cumsum(splits)])
    (s_ax, s_ag, s_qkv, s_z, s_beta, s_alpha, s_cq, s_ck, s_cv, s_cg, s_cr, s_merge) = [
        slice(int(offs[i]), int(offs[i + 1])) for i in range(len(splits))]
    n_small = 2 * B_HEADS + c_rank
    w_in_b = _mx(w_in)
    w_small = jnp.concatenate(
        [w_in_b[:, :, s_beta], w_in_b[:, :, s_alpha], w_in_b[:, :, s_cg],
         jnp.zeros((depth, D, SMALL_W - n_small), MXU_DTYPE)], axis=-1)

    def cols(*groups):
        return jnp.concatenate([g if not isinstance(g, slice) else w_in_b[:, :, g] for g in groups],
                               axis=-1)

    w_a = cols(s_ax, s_ag)
    w_b = cols(s_qkv, s_z, w_small)
    w_c = cols(s_cq, s_ck, s_cv, s_cr, w_small)
    w_m = cols(s_merge)

    eye_blocks = jnp.eye(A_BLOCKS, dtype=F32)

    def block_diag(w):
        return jnp.einsum("lhij,hg->lhigj", w, eye_blocks).reshape(depth, a_width, a_width)

    wr_bd = block_diag(a_w_r).astype(MXU_DTYPE)
    wi_bd = block_diag(a_w_i).astype(MXU_DTYPE)
    wg_pad = jnp.zeros((depth, SMALL_W, c_kw), F32).at[:, 2 * B_HEADS:n_small, :].set(c_w_g2)
    wg_pad = wg_pad.astype(MXU_DTYPE)
    wb = w_branch.astype(MXU_DTYPE)
    wo = w_out.astype(MXU_DTYPE)
    w1 = ffn_w1.astype(MXU_DTYPE)
    w3 = ffn_w3.astype(MXU_DTYPE)
    w2 = ffn_w2.astype(MXU_DTYPE)

    ts = _tile(S, 512)
    nt = S // ts
    n_tiles = B * nt
    assert ts % GDN_CHUNK == 0 and ts % GLA_ATTN_TILE == 0 and GLA_ATTN_TILE % GLA_CHUNK == 0
    tm_merge = _tile(T, 1024)
    sub_merge = _tile(tm_merge, 256)
    tm_ffn = _tile(T, 1024)
    sub_ffn = _tile(tm_ffn, 256)

    row = lambda v: v[:, None, :]
    alog_pad = _pad_lanes(b_a_log, B_HEADS, SMALL_W)
    dtb_pad = _pad_lanes(b_dt_bias, B_HEADS, SMALL_W)
    mix = dict(n_tiles=n_tiles, nt=nt, ts=ts)

    h = x.reshape(T, D)
    hb = _mx(h)
    for l in range(depth):
        ya = _rglru(hb, w_a, a_conv_w, row(a_conv_b), wr_bd, row(a_b_r), wi_bd, row(a_b_i),
                    row(a_lambda), layer=l, **mix)
        yb = _gdn(hb, w_b, b_conv_w, alog_pad, dtb_pad, row(b_norm_w), layer=l, **mix)
        yc = _gla(hb, w_c, wg_pad, row(c_b_g2), row(c_norm_w), layer=l, **mix)
        h, hb = _merge_out(ya, yb, yc, hb, h, w_m, wb, row(gate_b), wo, row(ln1_g), row(ln1_b),
                           layer=l, alpha=alpha, tm=tm_merge, sub=sub_merge)
        h, hb = _ffn(h, hb, w1, w3, w2, row(ln2_g), row(ln2_b),
                     layer=l, alpha=alpha, tm=tm_ffn, sub=sub_ffn)
    return h.reshape(B, S, D)
```

```python
import functools

import jax
import jax.numpy as jnp
import numpy as np
from jax import lax
from jax.experimental import pallas as pl
from jax.experimental.pallas import tpu as pltpu

F32 = jnp.float32
MXU_DTYPE = jnp.bfloat16

V7X_LANES = 128
V7X_SUBLANES = 8
V7X_VMEM_LIMIT_BYTES = 56 * 1024 * 1024

N_BRANCH = 3
A_BLOCKS = 8
A_C = 8.0
CONV_K = 4
B_HEADS = 4
C_HEADS = 4
C_TAU = 16.0
GLA_CHUNK = 64
GLA_ATTN_TILE = 256
GDN_CHUNK = 128
INV_BASE = 8
GDN_INV_LEVELS = (GDN_CHUNK // INV_BASE).bit_length() - 1
PROJ_PIECE = 256
GLA_PIECES_UP_FRONT = 3
LN_EPS = 1e-5
RMS_EPS = 1e-6
L2_EPS = 1e-6
SMALL_W = V7X_LANES
CONV_PAD = V7X_SUBLANES
OUT_ROWS = 2 * V7X_SUBLANES


def _mx(x):
    return x.astype(MXU_DTYPE)


def _dot(a, b):
    return jnp.dot(a, b, preferred_element_type=F32)


def _dot_nt(a, b):
    return lax.dot_general(a, b, (((1,), (1,)), ((), ())), preferred_element_type=F32)


def _dot_tn(a, b):
    return lax.dot_general(a, b, (((0,), (0,)), ((), ())), preferred_element_type=F32)


def _sigmoid(x):
    return 0.5 * jnp.tanh(0.5 * x) + 0.5


def _silu(x):
    hx = 0.5 * x
    return hx * jnp.tanh(hx) + hx


def _softplus(x):
    return jnp.maximum(x, 0.0) + jnp.log1p(jnp.exp(-jnp.abs(x)))


def _gelu_tanh(x):
    c = np.float32(np.sqrt(2.0 / np.pi))
    return 0.5 * x * (1.0 + jnp.tanh(c * (x + 0.044715 * (x * x * x))))


def _row_iota(shape):
    return lax.broadcasted_iota(jnp.int32, shape, 0)


def _col_iota(shape):
    return lax.broadcasted_iota(jnp.int32, shape, 1)


def _chunk_cumsum(x, chunk):
    pos = _row_iota(x.shape) % chunk
    d = 1
    while d < chunk:
        x = x + jnp.where(pos >= d, pltpu.roll(x, d, 0), 0.0)
        d *= 2
    return x


def _causal_conv(hist_ref, x_ref, col0, width, w_ref, first_tile):
    assert w_ref.shape[0] == CONV_K == 4
    ts = x_ref.shape[0]
    strips = []
    for c0 in range(0, width, V7X_LANES):
        cs = slice(c0, c0 + V7X_LANES)
        x = x_ref[:, col0 + c0:col0 + c0 + V7X_LANES]
        hist = jnp.where(first_tile, 0.0, hist_ref[:, cs])
        ext = jnp.concatenate([hist, x], axis=0)
        prev = pltpu.roll(ext, 1, 0)
        near = ext * w_ref[3:4, cs] + prev * w_ref[2:3, cs]
        far = ext * w_ref[1:2, cs] + prev * w_ref[0:1, cs]
        acc = near[CONV_PAD:CONV_PAD + ts, :] + pltpu.roll(far, 2, 0)[CONV_PAD:CONV_PAD + ts, :]
        hist_ref[:, cs] = x[ts - CONV_PAD:ts, :]
        strips.append(acc)
    return strips


def _lookahead_tile(nt):
    s = pl.program_id(0)
    tile = jnp.maximum(s - 1, 0)
    return lax.rem(tile, nt) == 0


def _zero_on_first_step(*refs):
    @pl.when(pl.program_id(0) == 0)
    def _():
        for ref in refs:
            ref[...] = jnp.zeros(ref.shape, ref.dtype)


class _Lookahead:
    def __init__(self, xb_ref, w_ref, pn_ref, cur_ref, piece_cols):
        cur_ref[...] = pn_ref[...]
        self._refs = (xb_ref, w_ref, pn_ref)
        width = w_ref.shape[1]
        self._todo = [(c0, min(c0 + piece_cols, width)) for c0 in range(0, width, piece_cols)]

    def issue(self, n=1):
        xb_ref, w_ref, pn_ref = self._refs
        for _ in range(min(n, len(self._todo))):
            c0, c1 = self._todo.pop(0)
            pn_ref[:, c0:c1] = _dot(xb_ref[...], w_ref[:, c0:c1])

    def finish(self):
        self.issue(len(self._todo))


def _mixer_call(body, xb, w, params, *, layer, n_tiles, ts, out_w, scratch, name):
    D = xb.shape[1]
    T = xb.shape[0]
    return pl.pallas_call(
        body,
        out_shape=jax.ShapeDtypeStruct((T, out_w), MXU_DTYPE),
        grid=(n_tiles + 1,),
        in_specs=[
            pl.BlockSpec((ts, D), lambda s: (jnp.minimum(s, n_tiles - 1), 0)),
            _layer_spec(w, layer),
        ] + [_layer_spec(p, layer) for p in params],
        out_specs=pl.BlockSpec((ts, out_w), lambda s: (jnp.maximum(s - 1, 0), 0)),
        scratch_shapes=[pltpu.VMEM((ts, w.shape[-1]), F32),
                        pltpu.VMEM((ts, w.shape[-1]), F32)
                        ] + scratch,
        compiler_params=pltpu.CompilerParams(
            dimension_semantics=("arbitrary",),
            vmem_limit_bytes=V7X_VMEM_LIMIT_BYTES),
        name=name,
    )(xb, w, *params)


def _layer_spec(a, layer):
    return pl.BlockSpec((None,) + a.shape[1:], lambda *_: (layer,) + (0,) * (a.ndim - 1))


def _rglru_kernel(nt, xb_ref, w_ref, cw_ref, cb_ref, wr_ref, br_ref, wi_ref, bi_ref, lam_ref,
                  o_ref, pn_ref, cur_ref, hist_ref, a_ref, u_ref, carry_ref):
    ts = xb_ref.shape[0]
    W = cw_ref.shape[1]
    first = _lookahead_tile(nt)
    _zero_on_first_step(pn_ref, hist_ref, carry_ref)
    look = _Lookahead(xb_ref, w_ref, pn_ref, cur_ref, w_ref.shape[1])
    look.finish()
    xa = jnp.concatenate(_causal_conv(hist_ref, cur_ref, 0, W, cw_ref, first), axis=-1) + cb_ref[...]
    xab = _mx(xa)
    r = _sigmoid(_dot(xab, wr_ref[...]) + br_ref[...])
    i = _sigmoid(_dot(xab, wi_ref[...]) + bi_ref[...])
    log_a = (-A_C) * r * _softplus(-lam_ref[...])
    a = jnp.exp(log_a)
    a_ref[...] = a
    u_ref[...] = jnp.sqrt(1.0 - a * a) * (i * xa)
    rows = _row_iota((V7X_SUBLANES, W))
    h0 = jnp.where(first, 0.0, carry_ref[...])

    def scan8(r0, h_prev):
        ca = a_ref[pl.ds(r0, V7X_SUBLANES), :]
        cb = u_ref[pl.ds(r0, V7X_SUBLANES), :]
        d = 1
        while d < V7X_SUBLANES:
            keep = rows >= d
            cb = jnp.where(keep, ca * pltpu.roll(cb, d, 0) + cb, cb)
            ca = jnp.where(keep, ca * pltpu.roll(ca, d, 0), ca)
            d *= 2
        h = ca * h_prev + cb
        return h, jnp.broadcast_to(h[V7X_SUBLANES - 1:V7X_SUBLANES, :], (V7X_SUBLANES, W))

    def body(blk, h_prev):
        r0 = pl.multiple_of(blk * OUT_ROWS, OUT_ROWS)
        hs = []
        for j in range(OUT_ROWS // V7X_SUBLANES):
            h, h_prev = scan8(r0 + j * V7X_SUBLANES, h_prev)
            hs.append(h)
        gate = _gelu_tanh(cur_ref[pl.ds(r0, OUT_ROWS), W:2 * W])
        o_ref[pl.ds(r0, OUT_ROWS), :] = (jnp.concatenate(hs, axis=0) * gate).astype(o_ref.dtype)
        return h_prev

    carry_ref[...] = lax.fori_loop(0, ts // OUT_ROWS, body, h0, unroll=2)


def _rglru(xb, w, cw, cb, wr, br, wi, bi, lam, *, layer, n_tiles, nt, ts):
    W = cw.shape[-1]
    params = (cw, cb, wr, br, wi, bi, lam)
    scratch = [
        pltpu.VMEM((CONV_PAD, W), F32),
        pltpu.VMEM((ts, W), F32),
        pltpu.VMEM((ts, W), F32),
        pltpu.VMEM((V7X_SUBLANES, W), F32),
    ]
    return _mixer_call(functools.partial(_rglru_kernel, nt), xb, w, params, layer=layer,
                       n_tiles=n_tiles, ts=ts, out_w=W, scratch=scratch, name="rglru")


def _gdn_kernel(nt, xb_ref, w_ref, cw_ref, alog_ref, dtb_ref, nw_ref,
                o_ref, pn_ref, cur_ref, hist_ref, state_ref, lmask_ref):
    ts = xb_ref.shape[0]
    C = GDN_CHUNK
    n_chunks = ts // C
    dk = V7X_LANES
    hw = B_HEADS * dk
    wq = cw_ref.shape[1]
    first = _lookahead_tile(nt)
    _zero_on_first_step(pn_ref, hist_ref, state_ref)

    @pl.when(pl.program_id(0) == 0)
    def _():
        ri0 = _row_iota((C, C))
        ci0 = _col_iota((C, C))
        for lv in range(lmask_ref.shape[0]):
            n = INV_BASE << lv
            lm = jnp.logical_and((ri0 // (2 * n)) == (ci0 // (2 * n)), (ri0 // n) != (ci0 // n))
            lmask_ref[lv] = lm.astype(F32).astype(MXU_DTYPE)

    look = _Lookahead(xb_ref, w_ref, pn_ref, cur_ref, w_ref.shape[1])
    look.finish()
    sm = cur_ref[:, wq + hw:wq + hw + SMALL_W]
    qkv = [_silu(s) for s in _causal_conv(hist_ref, cur_ref, 0, wq, cw_ref, first)]
    beta_all = _sigmoid(sm)
    g_all = -jnp.exp(alog_ref[...]) * _softplus(sm + dtb_ref[...])
    gcum_all = _chunk_cumsum(g_all, C)
    gcum_t = gcum_all.T

    ri = _row_iota((C, C))
    ci = _col_iota((C, C))
    lower = ci <= ri
    strict = ci < ri
    eye = (ci == ri).astype(F32)
    scale = np.float32(dk ** -0.5)
    base_mask = (ri // INV_BASE) == (ci // INV_BASE)
    level_masks = [lmask_ref[lv] for lv in range(lmask_ref.shape[0])]

    chains = [(c, h) for c in range(n_chunks) for h in range(B_HEADS)]
    st = {}
    for c, h in chains:
        rs = slice(c * C, (c + 1) * C)
        q = qkv[h][rs]
        k = qkv[B_HEADS + h][rs]
        v = qkv[2 * B_HEADS + h][rs]
        q = q * (lax.rsqrt(jnp.sum(q * q, -1, keepdims=True) + L2_EPS) * scale)
        k = k * lax.rsqrt(jnp.sum(k * k, -1, keepdims=True) + L2_EPS)
        beta = beta_all[rs, h:h + 1]
        gcol = gcum_all[rs, B_HEADS + h:B_HEADS + h + 1]
        grow = gcum_t[B_HEADS + h:B_HEADS + h + 1, rs]
        decay = jnp.where(lower, jnp.exp(jnp.where(lower, gcol - grow, 0.0)), 0.0)
        qb, kb = _mx(q), _mx(k)
        kk = _dot_nt(kb, kb)
        attn_b = _mx(_dot_nt(qb, kb) * decay)
        a_mat = jnp.where(strict, beta * kk * decay, 0.0)
        a_base = jnp.where(base_mask, a_mat, 0.0)
        st[c, h] = dict(q=q, k=k, v=v, beta=beta, gcol=gcol, attn_b=attn_b, a_b=_mx(a_mat),
                        p=eye - a_base, x=a_base)
    span = 2
    while span < INV_BASE:
        for key in chains:
            d = st[key]
            xb = _mx(d["x"])
            d["x"] = _dot(xb, xb)
        for key in chains:
            d = st[key]
            d["p"] = d["p"] + _dot(_mx(d["p"]), _mx(d["x"]))
        span *= 2
    for lm in level_masks:
        for key in chains:
            d = st[key]
            d["pb"] = _mx(d["p"])
            d["m"] = _dot(d["a_b"] * lm, d["pb"])
        for key in chains:
            d = st[key]
            d["p"] = d["p"] - _dot(d["pb"], _mx(d["m"]))
    for key in chains:
        d = st[key]
        e_g = jnp.exp(d["gcol"])
        rhs = jnp.concatenate([d["v"] * d["beta"], d["k"] * (d["beta"] * e_g)], axis=-1)
        d["solb"] = _mx(_dot(_mx(d["p"]), _mx(rhs)))
        g_last = d["gcol"][C - 1:C, :]
        d.update(q_g=d["q"] * e_g, kdb=_mx(d["k"] * jnp.exp(g_last - d["gcol"])),
                 dl=jnp.exp(g_last))
    for key in chains:
        d = st[key]
        kd_sol = _dot_tn(d["kdb"], d["solb"])
        at_sol = _dot(d["attn_b"], d["solb"])
        d.update(o0=at_sol[:, :dk], n_mat=kd_sol[:, :dk],
                 lhs=_mx(jnp.concatenate([d["q_g"] - at_sol[:, dk:], kd_sol[:, dk:]], axis=0)))
    states = [jnp.where(first, 0.0, state_ref[h]) for h in range(B_HEADS)]
    for c in range(n_chunks):
        rs = slice(c * C, (c + 1) * C)
        for h in range(B_HEADS):
            d = st[c, h]
            prod = _dot(d["lhs"], _mx(states[h]))
            o = prod[:C] + d["o0"]
            states[h] = states[h] * d["dl"] - prod[C:] + d["n_mat"]
            z = cur_ref[rs, wq + h * dk:wq + (h + 1) * dk]
            y = (o * lax.rsqrt(jnp.mean(o * o, -1, keepdims=True) + RMS_EPS) * nw_ref[...]
                 * _silu(z))
            o_ref[rs, h * dk:(h + 1) * dk] = y.astype(o_ref.dtype)
    for h in range(B_HEADS):
        state_ref[h] = states[h]


def _gdn(xb, w, cw, alog, dtb, nw, *, layer, n_tiles, nt, ts):
    wq = cw.shape[-1]
    wz = B_HEADS * V7X_LANES
    params = (cw, alog, dtb, nw)
    scratch = [
        pltpu.VMEM((CONV_PAD, wq), F32),
        pltpu.VMEM((B_HEADS, V7X_LANES, V7X_LANES), F32),
        pltpu.VMEM((GDN_INV_LEVELS, GDN_CHUNK, GDN_CHUNK), MXU_DTYPE),
    ]
    return _mixer_call(functools.partial(_gdn_kernel, nt), xb, w, params, layer=layer,
                       n_tiles=n_tiles, ts=ts, out_w=wz, scratch=scratch, name="gdn")


def _gla_kernel(nt, xb_ref, w_ref, wg_ref, bg_ref, nw_ref, o_ref, pn_ref, cur_ref, state_ref):
    ts = xb_ref.shape[0]
    C = GLA_CHUNK
    n_chunks = ts // C
    AT = GLA_ATTN_TILE
    kw = wg_ref.shape[1]
    dv = V7X_LANES
    wv = C_HEADS * dv
    pair_k = V7X_LANES
    dk = pair_k // 2
    n_pairs = kw // pair_k
    first = _lookahead_tile(nt)
    _zero_on_first_step(pn_ref, state_ref)
    look = _Lookahead(xb_ref, w_ref, pn_ref, cur_ref, PROJ_PIECE)
    sm = cur_ref[:, 2 * kw + 2 * wv:2 * kw + 2 * wv + SMALL_W]
    q = cur_ref[:, 0:kw] * np.float32(dk ** -0.5)
    k = cur_ref[:, kw:2 * kw]
    vb = _mx(cur_ref[:, 2 * kw:2 * kw + wv])

    zg = _dot(_mx(sm), wg_ref[...]) + bg_ref[...]
    look.issue(GLA_PIECES_UP_FRONT)
    log_a = -_softplus(-zg) * np.float32(1.0 / C_TAU)
    b_cum = _chunk_cumsum(log_a, C)
    q_in = q * jnp.exp(b_cum)
    kinb = _mx(k * jnp.exp(-b_cum))

    ri = _row_iota((AT, AT))
    ci = _col_iota((AT, AT))
    causal = jnp.logical_and(ci <= ri, (ri // C) == (ci // C))
    lane_head = _col_iota((AT, pair_k)) // dk
    bd_mask = (_row_iota((pair_k, 2 * dv)) // dk) == (_col_iota((pair_k, 2 * dv)) // dv)

    o_intra = {}
    for a in range(ts // AT):
        ra = slice(a * AT, (a + 1) * AT)
        for p in range(n_pairs):
            ks = slice(p * pair_k, (p + 1) * pair_k)
            for e in range(2):
                qm = _mx(jnp.where(lane_head == e, q_in[ra, ks], 0.0))
                attn = jnp.where(causal, _dot_nt(qm, kinb[ra, ks]), 0.0)
                hcol = slice((2 * p + e) * dv, (2 * p + e + 1) * dv)
                o_intra[a, 2 * p + e] = _dot(_mx(attn), vb[ra, hcol])
    o_inter = {}
    for p in range(n_pairs):
        ks = slice(p * pair_k, (p + 1) * pair_k)
        b_p = b_cum[:, ks]
        b_last_rows = [b_p[(c + 1) * C - 1:(c + 1) * C, :] for c in range(n_chunks)]
        b_last_blk = jnp.concatenate(
            b_last_rows + [jnp.zeros((V7X_LANES - n_chunks, pair_k), F32)], axis=0)
        d_last_cols = jnp.exp(b_last_blk).T
        upd = []
        for c in range(n_chunks):
            rs = slice(c * C, (c + 1) * C)
            k_dec = k[rs, ks] * jnp.exp(b_last_rows[c] - b_p[rs])
            upd.append(jnp.where(bd_mask, _dot_tn(_mx(k_dec), vb[rs, p * 2 * dv:(p + 1) * 2 * dv]), 0.0))
        state = jnp.where(first, 0.0, state_ref[p])
        qb_p = _mx(q_in[:, ks])
        for c in range(n_chunks):
            rs = slice(c * C, (c + 1) * C)
            o_inter[p, c] = _dot(qb_p[rs], _mx(state))
            state = state * d_last_cols[:, c:c + 1] + upd[c]
        state_ref[p] = state
    look.finish()
    for p in range(n_pairs):
        o_int = jnp.concatenate([o_inter[p, c] for c in range(n_chunks)], axis=0)
        for e in range(2):
            hd = 2 * p + e
            hcol = slice(hd * dv, (hd + 1) * dv)
            o = (jnp.concatenate([o_intra[a, hd] for a in range(ts // AT)], axis=0)
                 + o_int[:, e * dv:(e + 1) * dv])
            r_gate = _silu(cur_ref[:, 2 * kw + wv + hd * dv:2 * kw + wv + (hd + 1) * dv])
            y = (o * lax.rsqrt(jnp.mean(o * o, -1, keepdims=True) + RMS_EPS) * nw_ref[...]
                 * r_gate)
            o_ref[:, hcol] = y.astype(o_ref.dtype)


def _gla(xb, w, wg, bg, nw, *, layer, n_tiles, nt, ts):
    kw = wg.shape[-1]
    wv = C_HEADS * V7X_LANES
    params = (wg, bg, nw)
    scratch = [
        pltpu.VMEM((kw // V7X_LANES, V7X_LANES, 2 * V7X_LANES), F32),
    ]
    return _mixer_call(functools.partial(_gla_kernel, nt), xb, w, params, layer=layer,
                       n_tiles=n_tiles, ts=ts, out_w=wv, scratch=scratch, name="gla")


def _layer_norm(h, g, b):
    mu = jnp.mean(h, -1, keepdims=True)
    hc = h - mu
    var = jnp.mean(hc * hc, -1, keepdims=True)
    return hc * lax.rsqrt(var + LN_EPS) * g + b


def _merge_kernel(alpha, sub, ya_ref, yb_ref, yc_ref, xb_ref, x_ref, wm_ref, wb_ref, gb_ref,
                  wo_ref, g_ref, b_ref, o_ref, ob_ref):
    tm, D = x_ref.shape
    for r0 in range(0, tm, sub):
        rs = slice(r0, r0 + sub)
        xb = xb_ref[rs, :]
        merged = None
        for gi, y_ref in enumerate((ya_ref, yb_ref, yc_ref)):
            cs = slice(gi * D, (gi + 1) * D)
            gate = _sigmoid(_dot(xb, wm_ref[:, cs]) + gb_ref[:, cs])
            term = gate * _dot(y_ref[rs, :], wb_ref[gi])
            merged = term if merged is None else merged + term
        mix = _dot(_mx(merged), wo_ref[...])
        y = _layer_norm(alpha * x_ref[rs, :] + mix, g_ref[...], b_ref[...])
        o_ref[rs, :] = y
        ob_ref[rs, :] = _mx(y)


def _merge_out(ya, yb, yc, xb, x, wm, wb, gb, wo, ln_g, ln_b, *, layer, alpha, tm, sub):
    T, D = x.shape
    W = ya.shape[1]
    row = lambda w: pl.BlockSpec((tm, w), lambda i: (i, 0))
    params = (wm, wb, gb, wo, ln_g, ln_b)
    return pl.pallas_call(
        functools.partial(_merge_kernel, alpha, sub),
        out_shape=(jax.ShapeDtypeStruct((T, D), F32), jax.ShapeDtypeStruct((T, D), MXU_DTYPE)),
        grid=(T // tm,),
        in_specs=[row(W), row(W), row(W), row(D), row(D)] + [_layer_spec(p, layer) for p in params],
        out_specs=(row(D), row(D)),
        compiler_params=pltpu.CompilerParams(
            dimension_semantics=("parallel",),
            vmem_limit_bytes=V7X_VMEM_LIMIT_BYTES),
        name="merge_out",
    )(ya, yb, yc, xb, x, wm, wb, gb, wo, ln_g, ln_b)


def _ffn_kernel(alpha, sub, x_ref, xb_ref, w1_ref, w3_ref, w2_ref, g_ref, b_ref, o_ref, ob_ref):
    tm = x_ref.shape[0]
    for r0 in range(0, tm, sub):
        rs = slice(r0, r0 + sub)
        xb = xb_ref[rs, :]
        h = _silu(_dot(xb, w1_ref[...])) * _dot(xb, w3_ref[...])
        y = _layer_norm(alpha * x_ref[rs, :] + _dot(_mx(h), w2_ref[...]), g_ref[...], b_ref[...])
        o_ref[rs, :] = y
        ob_ref[rs, :] = _mx(y)


def _ffn(x, xb, w1, w3, w2, ln_g, ln_b, *, layer, alpha, tm, sub):
    T, D = x.shape
    row = pl.BlockSpec((tm, D), lambda i: (i, 0))
    return pl.pallas_call(
        functools.partial(_ffn_kernel, alpha, sub),
        out_shape=(jax.ShapeDtypeStruct((T, D), F32), jax.ShapeDtypeStruct((T, D), MXU_DTYPE)),
        grid=(T // tm,),
        in_specs=[row, row] + [_layer_spec(p, layer) for p in (w1, w3, w2, ln_g, ln_b)],
        out_specs=(row, row),
        compiler_params=pltpu.CompilerParams(
            dimension_semantics=("parallel",),
            vmem_limit_bytes=V7X_VMEM_LIMIT_BYTES),
        name="ffn",
    )(x, xb, w1, w3, w2, ln_g, ln_b)


def _tile(n, pref):
    t = min(n, pref)
    while n % t:
        t -= 1
    return t


def _pad_lanes(v, offset, width):
    return jnp.zeros((v.shape[0], 1, width), F32).at[:, 0, offset:offset + v.shape[1]].set(v)


def kernel(x, w_in, a_conv_w, a_conv_b, a_w_r, a_b_r, a_w_i, a_b_i, a_lambda, b_conv_w, b_a_log,
           b_dt_bias, b_norm_w, c_w_g2, c_b_g2, c_norm_w, gate_b, w_branch, w_out, ln1_g, ln1_b,
           ffn_w1, ffn_w3, ffn_w2, ln2_g, ln2_b):
    B, S, D = x.shape
    depth = w_in.shape[0]
    T = B * S
    a_width = a_conv_w.shape[2]
    qkv_w = b_conv_w.shape[2]
    b_width = B_HEADS * V7X_LANES
    c_kw = c_w_g2.shape[2]
    c_rank = c_w_g2.shape[1]
    c_width = C_HEADS * V7X_LANES
    alpha = float((2.0 * depth) ** 0.25)

    splits = (a_width, a_width, qkv_w, b_width, B_HEADS, B_HEADS, c_kw, c_kw, c_width, c_rank,
              c_width, N_BRANCH * D)
    offs = np.concatenate([[0], np.cumsum(splits)])
    (s_ax, s_ag, s_qkv, s_z, s_beta, s_alpha, s_cq, s_ck, s_cv, s_cg, s_cr, s_merge) = [
        slice(int(offs[i]), int(offs[i + 1])) for i in range(len(splits))]
    n_small = 2 * B_HEADS + c_rank
    w_in_b = _mx(w_in)
    w_small = jnp.concatenate(
        [w_in_b[:, :, s_beta], w_in_b[:, :, s_alpha], w_in_b[:, :, s_cg],
         jnp.zeros((depth, D, SMALL_W - n_small), MXU_DTYPE)], axis=-1)

    def cols(*groups):
        return jnp.concatenate([g if not isinstance(g, slice) else w_in_b[:, :, g] for g in groups],
                               axis=-1)

    w_a = cols(s_ax, s_ag)
    w_b = cols(s_qkv, s_z, w_small)
    w_c = cols(s_cq, s_ck, s_cv, s_cr, w_small)
    w_m = cols(s_merge)

    eye_blocks = jnp.eye(A_BLOCKS, dtype=F32)

    def block_diag(w):
        return jnp.einsum("lhij,hg->lhigj", w, eye_blocks).reshape(depth, a_width, a_width)

    wr_bd = block_diag(a_w_r).astype(MXU_DTYPE)
    wi_bd = block_diag(a_w_i).astype(MXU_DTYPE)
    wg_pad = jnp.zeros((depth, SMALL_W, c_kw), F32).at[:, 2 * B_HEADS:n_small, :].set(c_w_g2)
    wg_pad = wg_pad.astype(MXU_DTYPE)
    wb = w_branch.astype(MXU_DTYPE)
    wo = w_out.astype(MXU_DTYPE)
    w1 = ffn_w1.astype(MXU_DTYPE)
    w3 = ffn_w3.astype(MXU_DTYPE)
    w2 = ffn_w2.astype(MXU_DTYPE)

    ts = _tile(S, 512)
    nt = S // ts
    n_tiles = B * nt
    assert ts % GDN_CHUNK == 0 and ts % GLA_ATTN_TILE == 0 and GLA_ATTN_TILE % GLA_CHUNK == 0
    tm_merge = _tile(T, 1024)
    sub_merge = _tile(tm_merge, 256)
    tm_ffn = _tile(T, 1024)
    sub_ffn = _tile(tm_ffn, 256)

    row = lambda v: v[:, None, :]
    alog_pad = _pad_lanes(b_a_log, B_HEADS, SMALL_W)
    dtb_pad = _pad_lanes(b_dt_bias, B_HEADS, SMALL_W)
    mix = dict(n_tiles=n_tiles, nt=nt, ts=ts)

    h = x.reshape(T, D)
    hb = _mx(h)
    for l in range(depth):
        ya = _rglru(hb, w_a, a_conv_w, row(a_conv_b), wr_bd, row(a_b_r), wi_bd, row(a_b_i),
                    row(a_lambda), layer=l, **mix)
        yb = _gdn(hb, w_b, b_conv_w, alog_pad, dtb_pad, row(b_norm_w), layer=l, **mix)
        yc = _gla(hb, w_c, wg_pad, row(c_b_g2), row(c_norm_w), layer=l, **mix)
        h, hb = _merge_out(ya, yb, yc, hb, h, w_m, wb, row(gate_b), wo, row(ln1_g), row(ln1_b),
                           layer=l, alpha=alpha, tm=tm_merge, sub=sub_merge)
        h, hb = _ffn(h, hb, w1, w3, w2, row(ln2_g), row(ln2_b),
                     layer=l, alpha=alpha, tm=tm_ffn, sub=sub_ffn)
    return h.reshape(B, S, D)
```

```python
import functools

import jax
import jax.numpy as jnp
import numpy as np
from jax import lax
from jax.experimental import pallas as pl
from jax.experimental.pallas import tpu as pltpu

F32 = jnp.float32
MXU_DTYPE = jnp.bfloat16

V7X_LANES = 128
V7X_SUBLANES = 8
V7X_VMEM_LIMIT_BYTES = 56 * 1024 * 1024

N_BRANCH = 3
A_BLOCKS = 8
A_C = 8.0
CONV_K = 4
B_HEADS = 4
C_HEADS = 4
C_TAU = 16.0
GLA_CHUNK = 64
GLA_ATTN_TILE = 256
GDN_CHUNK = 128
INV_BASE = 8
GDN_INV_LEVELS = (GDN_CHUNK // INV_BASE).bit_length() - 1
PROJ_PIECE = 256
GLA_PIECES_UP_FRONT = 3
LN_EPS = 1e-5
RMS_EPS = 1e-6
L2_EPS = 1e-6
SQRT_GUARD = 1e-30
SMALL_W = V7X_LANES
CONV_PAD = V7X_SUBLANES
OUT_ROWS = 2 * V7X_SUBLANES


def _mx(x):
    return x.astype(MXU_DTYPE)


def _dot(a, b):
    return jnp.dot(a, b, preferred_element_type=F32)


def _dot_nt(a, b):
    return lax.dot_general(a, b, (((1,), (1,)), ((), ())), preferred_element_type=F32)


def _dot_tn(a, b):
    return lax.dot_general(a, b, (((0,), (0,)), ((), ())), preferred_element_type=F32)


def _sigmoid(x):
    return 0.5 * jnp.tanh(0.5 * x) + 0.5


def _silu(x):
    hx = 0.5 * x
    return hx * jnp.tanh(hx) + hx


def _softplus(x):
    return jnp.maximum(x, 0.0) + jnp.log1p(jnp.exp(-jnp.abs(x)))


def _gelu_tanh(x):
    c = np.float32(np.sqrt(2.0 / np.pi))
    c3 = np.float32(np.sqrt(2.0 / np.pi) * 0.044715)
    hx = 0.5 * x
    return hx + hx * jnp.tanh(x * (c + c3 * (x * x)))


def _row_iota(shape):
    return lax.broadcasted_iota(jnp.int32, shape, 0)


def _col_iota(shape):
    return lax.broadcasted_iota(jnp.int32, shape, 1)


def _chunk_cumsum(x, chunk):
    pos = _row_iota(x.shape) % chunk
    d = 1
    while d < chunk:
        x = x + jnp.where(pos >= d, pltpu.roll(x, d, 0), 0.0)
        d *= 2
    return x


def _causal_conv(hist_ref, x_ref, col0, width, w_ref, first_tile):
    assert w_ref.shape[0] == CONV_K == 4
    ts = x_ref.shape[0]
    strips = []
    for c0 in range(0, width, V7X_LANES):
        cs = slice(c0, c0 + V7X_LANES)
        x = x_ref[:, col0 + c0:col0 + c0 + V7X_LANES]
        hist = jnp.where(first_tile, 0.0, hist_ref[:, cs])
        ext = jnp.concatenate([hist, x], axis=0)
        prev = pltpu.roll(ext, 1, 0)
        near = ext * w_ref[3:4, cs] + prev * w_ref[2:3, cs]
        far = ext * w_ref[1:2, cs] + prev * w_ref[0:1, cs]
        acc = near[CONV_PAD:CONV_PAD + ts, :] + pltpu.roll(far, 2, 0)[CONV_PAD:CONV_PAD + ts, :]
        hist_ref[:, cs] = x[ts - CONV_PAD:ts, :]
        strips.append(acc)
    return strips


def _lookahead_tile(nt):
    s = pl.program_id(0)
    tile = jnp.maximum(s - 1, 0)
    return lax.rem(tile, nt) == 0


def _zero_on_first_step(*refs):
    @pl.when(pl.program_id(0) == 0)
    def _():
        for ref in refs:
            ref[...] = jnp.zeros(ref.shape, ref.dtype)


class _Lookahead:
    def __init__(self, xb_ref, w_ref, pn_ref, cur_ref, piece_cols):
        cur_ref[...] = pn_ref[...]
        self._refs = (xb_ref, w_ref, pn_ref)
        width = w_ref.shape[1]
        self._todo = [(c0, min(c0 + piece_cols, width)) for c0 in range(0, width, piece_cols)]

    def issue(self, n=1):
        xb_ref, w_ref, pn_ref = self._refs
        for _ in range(min(n, len(self._todo))):
            c0, c1 = self._todo.pop(0)
            pn_ref[:, c0:c1] = _dot(xb_ref[...], w_ref[:, c0:c1])

    def finish(self):
        self.issue(len(self._todo))


def _mixer_call(body, xb, w, params, *, layer, n_tiles, ts, out_w, scratch, name):
    D = xb.shape[1]
    T = xb.shape[0]
    return pl.pallas_call(
        body,
        out_shape=jax.ShapeDtypeStruct((T, out_w), MXU_DTYPE),
        grid=(n_tiles + 1,),
        in_specs=[
            pl.BlockSpec((ts, D), lambda s: (jnp.minimum(s, n_tiles - 1), 0)),
            _layer_spec(w, layer),
        ] + [_layer_spec(p, layer) for p in params],
        out_specs=pl.BlockSpec((ts, out_w), lambda s: (jnp.maximum(s - 1, 0), 0)),
        scratch_shapes=[pltpu.VMEM((ts, w.shape[-1]), F32),
                        pltpu.VMEM((ts, w.shape[-1]), F32)
                        ] + scratch,
        compiler_params=pltpu.CompilerParams(
            dimension_semantics=("arbitrary",),
            vmem_limit_bytes=V7X_VMEM_LIMIT_BYTES),
        name=name,
    )(xb, w, *params)


def _layer_spec(a, layer):
    return pl.BlockSpec((None,) + a.shape[1:], lambda *_: (layer,) + (0,) * (a.ndim - 1))


def _rglru_kernel(nt, xb_ref, w_ref, cw_ref, cb_ref, wr_ref, br_ref, wi_ref, bi_ref, lam_ref,
                  o_ref, pn_ref, cur_ref, hist_ref, a_ref, u_ref, carry_ref):
    ts = xb_ref.shape[0]
    W = cw_ref.shape[1]
    first = _lookahead_tile(nt)
    _zero_on_first_step(pn_ref, hist_ref, carry_ref)
    look = _Lookahead(xb_ref, w_ref, pn_ref, cur_ref, w_ref.shape[1])
    look.finish()
    xa = jnp.concatenate(_causal_conv(hist_ref, cur_ref, 0, W, cw_ref, first), axis=-1) + cb_ref[...]
    xab = _mx(xa)
    r = _sigmoid(_dot(xab, wr_ref[...]) + br_ref[...])
    i = _sigmoid(_dot(xab, wi_ref[...]) + bi_ref[...])
    log_a = (-A_C) * r * _softplus(-lam_ref[...])
    a = jnp.exp(log_a)
    a_ref[...] = a
    one_m_a2 = 1.0 - a * a
    u_ref[...] = (one_m_a2 * lax.rsqrt(jnp.maximum(one_m_a2, SQRT_GUARD))) * (i * xa)
    rows = _row_iota((V7X_SUBLANES, W))
    h0 = jnp.where(first, 0.0, carry_ref[...])

    def scan8(r0, h_prev):
        ca = a_ref[pl.ds(r0, V7X_SUBLANES), :]
        cb = u_ref[pl.ds(r0, V7X_SUBLANES), :]
        d = 1
        while d < V7X_SUBLANES:
            keep = rows >= d
            cb = jnp.where(keep, ca * pltpu.roll(cb, d, 0) + cb, cb)
            ca = jnp.where(keep, ca * pltpu.roll(ca, d, 0), ca)
            d *= 2
        h = ca * h_prev + cb
        return h, jnp.broadcast_to(h[V7X_SUBLANES - 1:V7X_SUBLANES, :], (V7X_SUBLANES, W))

    def body(blk, h_prev):
        r0 = pl.multiple_of(blk * OUT_ROWS, OUT_ROWS)
        hs = []
        for j in range(OUT_ROWS // V7X_SUBLANES):
            h, h_prev = scan8(r0 + j * V7X_SUBLANES, h_prev)
            hs.append(h)
        gate = _gelu_tanh(cur_ref[pl.ds(r0, OUT_ROWS), W:2 * W])
        o_ref[pl.ds(r0, OUT_ROWS), :] = (jnp.concatenate(hs, axis=0) * gate).astype(o_ref.dtype)
        return h_prev

    carry_ref[...] = lax.fori_loop(0, ts // OUT_ROWS, body, h0, unroll=2)


def _rglru(xb, w, cw, cb, wr, br, wi, bi, lam, *, layer, n_tiles, nt, ts):
    W = cw.shape[-1]
    params = (cw, cb, wr, br, wi, bi, lam)
    scratch = [
        pltpu.VMEM((CONV_PAD, W), F32),
        pltpu.VMEM((ts, W), F32),
        pltpu.VMEM((ts, W), F32),
        pltpu.VMEM((V7X_SUBLANES, W), F32),
    ]
    return _mixer_call(functools.partial(_rglru_kernel, nt), xb, w, params, layer=layer,
                       n_tiles=n_tiles, ts=ts, out_w=W, scratch=scratch, name="rglru")


def _gdn_kernel(nt, xb_ref, w_ref, cw_ref, alog_ref, dtb_ref, nw_ref,
                o_ref, pn_ref, cur_ref, hist_ref, state_ref, lmask_ref):
    ts = xb_ref.shape[0]
    C = GDN_CHUNK
    n_chunks = ts // C
    dk = V7X_LANES
    hw = B_HEADS * dk
    wq = cw_ref.shape[1]
    first = _lookahead_tile(nt)
    _zero_on_first_step(pn_ref, hist_ref, state_ref)

    @pl.when(pl.program_id(0) == 0)
    def _():
        ri0 = _row_iota((C, C))
        ci0 = _col_iota((C, C))
        for lv in range(lmask_ref.shape[0]):
            n = INV_BASE << lv
            lm = jnp.logical_and((ri0 // (2 * n)) == (ci0 // (2 * n)), (ri0 // n) != (ci0 // n))
            lmask_ref[lv] = lm.astype(F32).astype(MXU_DTYPE)

    look = _Lookahead(xb_ref, w_ref, pn_ref, cur_ref, w_ref.shape[1])
    look.finish()
    sm = cur_ref[:, wq + hw:wq + hw + SMALL_W]
    qkv = [_silu(s) for s in _causal_conv(hist_ref, cur_ref, 0, wq, cw_ref, first)]
    beta_all = _sigmoid(sm)
    g_all = -jnp.exp(alog_ref[...]) * _softplus(sm + dtb_ref[...])
    gcum_all = _chunk_cumsum(g_all, C)
    gcum_t = gcum_all.T

    ri = _row_iota((C, C))
    ci = _col_iota((C, C))
    lower = ci <= ri
    strict = ci < ri
    eye = (ci == ri).astype(F32)
    scale = np.float32(dk ** -0.5)
    base_mask = (ri // INV_BASE) == (ci // INV_BASE)
    level_masks = [lmask_ref[lv] for lv in range(lmask_ref.shape[0])]

    chains = [(c, h) for c in range(n_chunks) for h in range(B_HEADS)]
    st = {}
    for c, h in chains:
        rs = slice(c * C, (c + 1) * C)
        q = qkv[h][rs]
        k = qkv[B_HEADS + h][rs]
        v = qkv[2 * B_HEADS + h][rs]
        q = q * (lax.rsqrt(jnp.sum(q * q, -1, keepdims=True) + L2_EPS) * scale)
        k = k * lax.rsqrt(jnp.sum(k * k, -1, keepdims=True) + L2_EPS)
        beta = beta_all[rs, h:h + 1]
        gcol = gcum_all[rs, B_HEADS + h:B_HEADS + h + 1]
        grow = gcum_t[B_HEADS + h:B_HEADS + h + 1, rs]
        decay = jnp.where(lower, jnp.exp(jnp.where(lower, gcol - grow, 0.0)), 0.0)
        qb, kb = _mx(q), _mx(k)
        kk = _dot_nt(kb, kb)
        attn_b = _mx(_dot_nt(qb, kb) * decay)
        a_mat = jnp.where(strict, beta * kk * decay, 0.0)
        a_base = jnp.where(base_mask, a_mat, 0.0)
        st[c, h] = dict(q=q, k=k, v=v, beta=beta, gcol=gcol, attn_b=attn_b, a_b=_mx(a_mat),
                        p=eye - a_base, x=a_base)
    span = 2
    while span < INV_BASE:
        for key in chains:
            d = st[key]
            xb = _mx(d["x"])
            d["x"] = _dot(xb, xb)
        for key in chains:
            d = st[key]
            d["p"] = d["p"] + _dot(_mx(d["p"]), _mx(d["x"]))
        span *= 2
    for lm in level_masks:
        for key in chains:
            d = st[key]
            d["pb"] = _mx(d["p"])
            d["m"] = _dot(d["a_b"] * lm, d["pb"])
        for key in chains:
            d = st[key]
            d["p"] = d["p"] - _dot(d["pb"], _mx(d["m"]))
    for key in chains:
        d = st[key]
        e_g = jnp.exp(d["gcol"])
        rhs = jnp.concatenate([d["v"] * d["beta"], d["k"] * (d["beta"] * e_g)], axis=-1)
        d["solb"] = _mx(_dot(_mx(d["p"]), _mx(rhs)))
        g_last = d["gcol"][C - 1:C, :]
        d.update(q_g=d["q"] * e_g, kdb=_mx(d["k"] * jnp.exp(g_last - d["gcol"])),
                 dl=jnp.exp(g_last))
    for key in chains:
        d = st[key]
        kd_sol = _dot_tn(d["kdb"], d["solb"])
        at_sol = _dot(d["attn_b"], d["solb"])
        d.update(o0=at_sol[:, :dk], n_mat=kd_sol[:, :dk],
                 lhs=_mx(jnp.concatenate([d["q_g"] - at_sol[:, dk:], kd_sol[:, dk:]], axis=0)))
    states = [jnp.where(first, 0.0, state_ref[h]) for h in range(B_HEADS)]
    for c in range(n_chunks):
        rs = slice(c * C, (c + 1) * C)
        for h in range(B_HEADS):
            d = st[c, h]
            prod = _dot(d["lhs"], _mx(states[h]))
            o = prod[:C] + d["o0"]
            states[h] = states[h] * d["dl"] - prod[C:] + d["n_mat"]
            z = cur_ref[rs, wq + h * dk:wq + (h + 1) * dk]
            y = (o * lax.rsqrt(jnp.mean(o * o, -1, keepdims=True) + RMS_EPS) * nw_ref[...]
                 * _silu(z))
            o_ref[rs, h * dk:(h + 1) * dk] = y.astype(o_ref.dtype)
    for h in range(B_HEADS):
        state_ref[h] = states[h]


def _gdn(xb, w, cw, alog, dtb, nw, *, layer, n_tiles, nt, ts):
    wq = cw.shape[-1]
    wz = B_HEADS * V7X_LANES
    params = (cw, alog, dtb, nw)
    scratch = [
        pltpu.VMEM((CONV_PAD, wq), F32),
        pltpu.VMEM((B_HEADS, V7X_LANES, V7X_LANES), F32),
        pltpu.VMEM((GDN_INV_LEVELS, GDN_CHUNK, GDN_CHUNK), MXU_DTYPE),
    ]
    return _mixer_call(functools.partial(_gdn_kernel, nt), xb, w, params, layer=layer,
                       n_tiles=n_tiles, ts=ts, out_w=wz, scratch=scratch, name="gdn")


def _gla_kernel(nt, xb_ref, w_ref, wg_ref, bg_ref, nw_ref, o_ref, pn_ref, cur_ref, state_ref):
    ts = xb_ref.shape[0]
    C = GLA_CHUNK
    n_chunks = ts // C
    AT = GLA_ATTN_TILE
    kw = wg_ref.shape[1]
    dv = V7X_LANES
    wv = C_HEADS * dv
    pair_k = V7X_LANES
    dk = pair_k // 2
    n_pairs = kw // pair_k
    first = _lookahead_tile(nt)
    _zero_on_first_step(pn_ref, state_ref)
    look = _Lookahead(xb_ref, w_ref, pn_ref, cur_ref, PROJ_PIECE)
    sm = cur_ref[:, 2 * kw + 2 * wv:2 * kw + 2 * wv + SMALL_W]
    q = cur_ref[:, 0:kw] * np.float32(dk ** -0.5)
    k = cur_ref[:, kw:2 * kw]
    vb = _mx(cur_ref[:, 2 * kw:2 * kw + wv])

    zg = _dot(_mx(sm), wg_ref[...]) + bg_ref[...]
    look.issue(GLA_PIECES_UP_FRONT)
    log_a = -_softplus(-zg) * np.float32(1.0 / C_TAU)
    b_cum = _chunk_cumsum(log_a, C)
    q_in = q * jnp.exp(b_cum)
    kinb = _mx(k * jnp.exp(-b_cum))

    ri = _row_iota((AT, AT))
    ci = _col_iota((AT, AT))
    causal = jnp.logical_and(ci <= ri, (ri // C) == (ci // C))
    lane_head = _col_iota((AT, pair_k)) // dk
    bd_mask = (_row_iota((pair_k, 2 * dv)) // dk) == (_col_iota((pair_k, 2 * dv)) // dv)

    o_intra = {}
    for a in range(ts // AT):
        ra = slice(a * AT, (a + 1) * AT)
        for p in range(n_pairs):
            ks = slice(p * pair_k, (p + 1) * pair_k)
            for e in range(2):
                qm = _mx(jnp.where(lane_head == e, q_in[ra, ks], 0.0))
                attn = jnp.where(causal, _dot_nt(qm, kinb[ra, ks]), 0.0)
                hcol = slice((2 * p + e) * dv, (2 * p + e + 1) * dv)
                o_intra[a, 2 * p + e] = _dot(_mx(attn), vb[ra, hcol])
    o_inter = {}
    for p in range(n_pairs):
        ks = slice(p * pair_k, (p + 1) * pair_k)
        b_p = b_cum[:, ks]
        b_last_rows = [b_p[(c + 1) * C - 1:(c + 1) * C, :] for c in range(n_chunks)]
        b_last_blk = jnp.concatenate(
            b_last_rows + [jnp.zeros((V7X_LANES - n_chunks, pair_k), F32)], axis=0)
        d_last_cols = jnp.exp(b_last_blk).T
        upd = []
        for c in range(n_chunks):
            rs = slice(c * C, (c + 1) * C)
            k_dec = k[rs, ks] * jnp.exp(b_last_rows[c] - b_p[rs])
            upd.append(jnp.where(bd_mask, _dot_tn(_mx(k_dec), vb[rs, p * 2 * dv:(p + 1) * 2 * dv]), 0.0))
        state = jnp.where(first, 0.0, state_ref[p])
        qb_p = _mx(q_in[:, ks])
        for c in range(n_chunks):
            rs = slice(c * C, (c + 1) * C)
            o_inter[p, c] = _dot(qb_p[rs], _mx(state))
            state = state * d_last_cols[:, c:c + 1] + upd[c]
        state_ref[p] = state
    look.finish()
    for p in range(n_pairs):
        o_int = jnp.concatenate([o_inter[p, c] for c in range(n_chunks)], axis=0)
        for e in range(2):
            hd = 2 * p + e
            hcol = slice(hd * dv, (hd + 1) * dv)
            o = (jnp.concatenate([o_intra[a, hd] for a in range(ts // AT)], axis=0)
                 + o_int[:, e * dv:(e + 1) * dv])
            r_gate = _silu(cur_ref[:, 2 * kw + wv + hd * dv:2 * kw + wv + (hd + 1) * dv])
            y = (o * lax.rsqrt(jnp.mean(o * o, -1, keepdims=True) + RMS_EPS) * nw_ref[...]
                 * r_gate)
            o_ref[:, hcol] = y.astype(o_ref.dtype)


def _gla(xb, w, wg, bg, nw, *, layer, n_tiles, nt, ts):
    kw = wg.shape[-1]
    wv = C_HEADS * V7X_LANES
    params = (wg, bg, nw)
    scratch = [
        pltpu.VMEM((kw // V7X_LANES, V7X_LANES, 2 * V7X_LANES), F32),
    ]
    return _mixer_call(functools.partial(_gla_kernel, nt), xb, w, params, layer=layer,
                       n_tiles=n_tiles, ts=ts, out_w=wv, scratch=scratch, name="gla")


def _layer_norm(h, g, b):
    mu = jnp.mean(h, -1, keepdims=True)
    hc = h - mu
    var = jnp.mean(hc * hc, -1, keepdims=True)
    return hc * lax.rsqrt(var + LN_EPS) * g + b


def _merge_kernel(alpha, sub, ya_ref, yb_ref, yc_ref, xb_ref, x_ref, wm_ref, wb_ref, gb_ref,
                  wo_ref, g_ref, b_ref, o_ref, ob_ref):
    tm, D = x_ref.shape
    for r0 in range(0, tm, sub):
        rs = slice(r0, r0 + sub)
        xb = xb_ref[rs, :]
        merged = None
        for gi, y_ref in enumerate((ya_ref, yb_ref, yc_ref)):
            cs = slice(gi * D, (gi + 1) * D)
            gate = _sigmoid(_dot(xb, wm_ref[:, cs]) + gb_ref[:, cs])
            term = gate * _dot(y_ref[rs, :], wb_ref[gi])
            merged = term if merged is None else merged + term
        mix = _dot(_mx(merged), wo_ref[...])
        y = _layer_norm(alpha * x_ref[rs, :] + mix, g_ref[...], b_ref[...])
        o_ref[rs, :] = y
        ob_ref[rs, :] = _mx(y)


def _merge_out(ya, yb, yc, xb, x, wm, wb, gb, wo, ln_g, ln_b, *, layer, alpha, tm, sub):
    T, D = x.shape
    W = ya.shape[1]
    row = lambda w: pl.BlockSpec((tm, w), lambda i: (i, 0))
    params = (wm, wb, gb, wo, ln_g, ln_b)
    return pl.pallas_call(
        functools.partial(_merge_kernel, alpha, sub),
        out_shape=(jax.ShapeDtypeStruct((T, D), F32), jax.ShapeDtypeStruct((T, D), MXU_DTYPE)),
        grid=(T // tm,),
        in_specs=[row(W), row(W), row(W), row(D), row(D)] + [_layer_spec(p, layer) for p in params],
        out_specs=(row(D), row(D)),
        compiler_params=pltpu.CompilerParams(
            dimension_semantics=("parallel",),
            vmem_limit_bytes=V7X_VMEM_LIMIT_BYTES),
        name="merge_out",
    )(ya, yb, yc, xb, x, wm, wb, gb, wo, ln_g, ln_b)


def _ffn_kernel(alpha, sub, x_ref, xb_ref, w1_ref, w3_ref, w2_ref, g_ref, b_ref, o_ref, ob_ref):
    tm = x_ref.shape[0]
    for r0 in range(0, tm, sub):
        rs = slice(r0, r0 + sub)
        xb = xb_ref[rs, :]
        h = _silu(_dot(xb, w1_ref[...])) * _dot(xb, w3_ref[...])
        y = _layer_norm(alpha * x_ref[rs, :] + _dot(_mx(h), w2_ref[...]), g_ref[...], b_ref[...])
        o_ref[rs, :] = y
        ob_ref[rs, :] = _mx(y)


def _ffn(x, xb, w1, w3, w2, ln_g, ln_b, *, layer, alpha, tm, sub):
    T, D = x.shape
    row = pl.BlockSpec((tm, D), lambda i: (i, 0))
    return pl.pallas_call(
        functools.partial(_ffn_kernel, alpha, sub),
        out_shape=(jax.ShapeDtypeStruct((T, D), F32), jax.ShapeDtypeStruct((T, D), MXU_DTYPE)),
        grid=(T // tm,),
        in_specs=[row, row] + [_layer_spec(p, layer) for p in (w1, w3, w2, ln_g, ln_b)],
        out_specs=(row, row),
        compiler_params=pltpu.CompilerParams(
            dimension_semantics=("parallel",),
            vmem_limit_bytes=V7X_VMEM_LIMIT_BYTES),
        name="ffn",
    )(x, xb, w1, w3, w2, ln_g, ln_b)


def _tile(n, pref):
    t = min(n, pref)
    while n % t:
        t -= 1
    return t


def _pad_lanes(v, offset, width):
    return jnp.zeros((v.shape[0], 1, width), F32).at[:, 0, offset:offset + v.shape[1]].set(v)


def kernel(x, w_in, a_conv_w, a_conv_b, a_w_r, a_b_r, a_w_i, a_b_i, a_lambda, b_conv_w, b_a_log,
           b_dt_bias, b_norm_w, c_w_g2, c_b_g2, c_norm_w, gate_b, w_branch, w_out, ln1_g, ln1_b,
           ffn_w1, ffn_w3, ffn_w2, ln2_g, ln2_b):
    B, S, D = x.shape
    depth = w_in.shape[0]
    T = B * S
    a_width = a_conv_w.shape[2]
    qkv_w = b_conv_w.shape[2]
    b_width = B_HEADS * V7X_LANES
    c_kw = c_w_g2.shape[2]
    c_rank = c_w_g2.shape[1]
    c_width = C_HEADS * V7X_LANES
    alpha = float((2.0 * depth) ** 0.25)

    splits = (a_width, a_width, qkv_w, b_width, B_HEADS, B_HEADS, c_kw, c_kw, c_width, c_rank,
              c_width, N_BRANCH * D)
    offs = np.concatenate([[0], np.cumsum(splits)])
    (s_ax, s_ag, s_qkv, s_z, s_beta, s_alpha, s_cq, s_ck, s_cv, s_cg, s_cr, s_merge) = [
        slice(int(offs[i]), int(offs[i + 1])) for i in range(len(splits))]
    n_small = 2 * B_HEADS + c_rank
    w_in_b = _mx(w_in)
    w_small = jnp.concatenate(
        [w_in_b[:, :, s_beta], w_in_b[:, :, s_alpha], w_in_b[:, :, s_cg],
         jnp.zeros((depth, D, SMALL_W - n_small), MXU_DTYPE)], axis=-1)

    def cols(*groups):
        return jnp.concatenate([g if not isinstance(g, slice) else w_in_b[:, :, g] for g in groups],
                               axis=-1)

    w_a = cols(s_ax, s_ag)
    w_b = cols(s_qkv, s_z, w_small)
    w_c = cols(s_cq, s_ck, s_cv, s_cr, w_small)
    w_m = cols(s_merge)

    eye_blocks = jnp.eye(A_BLOCKS, dtype=F32)

    def block_diag(w):
        return jnp.einsum("lhij,hg->lhigj", w, eye_blocks).reshape(depth, a_width, a_width)

    wr_bd = block_diag(a_w_r).astype(MXU_DTYPE)
    wi_bd = block_diag(a_w_i).astype(MXU_DTYPE)
    wg_pad = jnp.zeros((depth, SMALL_W, c_kw), F32).at[:, 2 * B_HEADS:n_small, :].set(c_w_g2)
    wg_pad = wg_pad.astype(MXU_DTYPE)
    wb = w_branch.astype(MXU_DTYPE)
    wo = w_out.astype(MXU_DTYPE)
    w1 = ffn_w1.astype(MXU_DTYPE)
    w3 = ffn_w3.astype(MXU_DTYPE)
    w2 = ffn_w2.astype(MXU_DTYPE)

    def mixer_tiles(pref):
        ts = _tile(S, pref)
        return dict(ts=ts, nt=S // ts, n_tiles=B * (S // ts))

    mix = mixer_tiles(512)
    mix_gla = mixer_tiles(1024)
    assert mix["ts"] % GDN_CHUNK == 0 and mix["ts"] % OUT_ROWS == 0
    assert mix_gla["ts"] % GLA_ATTN_TILE == 0 and GLA_ATTN_TILE % GLA_CHUNK == 0
    tm_merge = _tile(T, 1024)
    sub_merge = _tile(tm_merge, 256)
    tm_ffn = _tile(T, 1024)
    sub_ffn = _tile(tm_ffn, 256)

    row = lambda v: v[:, None, :]
    alog_pad = _pad_lanes(b_a_log, B_HEADS, SMALL_W)
    dtb_pad = _pad_lanes(b_dt_bias, B_HEADS, SMALL_W)

    h = x.reshape(T, D)
    hb = _mx(h)
    for l in range(depth):
        ya = _rglru(hb, w_a, a_conv_w, row(a_conv_b), wr_bd, row(a_b_r), wi_bd, row(a_b_i),
                    row(a_lambda), layer=l, **mix)
        yb = _gdn(hb, w_b, b_conv_w, alog_pad, dtb_pad, row(b_norm_w), layer=l, **mix)
        yc = _gla(hb, w_c, wg_pad, row(c_b_g2), row(c_norm_w), layer=l, **mix_gla)
        h, hb = _merge_out(ya, yb, yc, hb, h, w_m, wb, row(gate_b), wo, row(ln1_g), row(ln1_b),
                           layer=l, alpha=alpha, tm=tm_merge, sub=sub_merge)
        h, hb = _ffn(h, hb, w1, w3, w2, row(ln2_g), row(ln2_b),
                     layer=l, alpha=alpha, tm=tm_ffn, sub=sub_ffn)
    return h.reshape(B, S, D)
```

```python
import functools

import jax
import jax.numpy as jnp
import numpy as np
from jax import lax
from jax.experimental import pallas as pl
from jax.experimental.pallas import tpu as pltpu

F32 = jnp.float32
MXU_DTYPE = jnp.bfloat16

V7X_LANES = 128
V7X_SUBLANES = 8
V7X_VMEM_LIMIT_BYTES = 56 * 1024 * 1024

N_BRANCH = 3
A_BLOCKS = 8
A_C = 8.0
CONV_K = 4
B_HEADS = 4
C_HEADS = 4
C_TAU = 16.0
GLA_CHUNK = 64
GLA_ATTN_TILE = 256
GDN_CHUNK = 128
INV_BASE = 8
GDN_INV_LEVELS = (GDN_CHUNK // INV_BASE).bit_length() - 1
PROJ_PIECE = 256
GLA_PIECES_UP_FRONT = 3
LN_EPS = 1e-5
RMS_EPS = 1e-6
L2_EPS = 1e-6
SQRT_GUARD = 1e-30
SMALL_W = V7X_LANES
CONV_PAD = V7X_SUBLANES
OUT_ROWS = 2 * V7X_SUBLANES


def _mx(x):
    return x.astype(MXU_DTYPE)


def _dot(a, b):
    return jnp.dot(a, b, preferred_element_type=F32)


def _dot_nt(a, b):
    return lax.dot_general(a, b, (((1,), (1,)), ((), ())), preferred_element_type=F32)


def _dot_tn(a, b):
    return lax.dot_general(a, b, (((0,), (0,)), ((), ())), preferred_element_type=F32)


def _sigmoid(x):
    return 0.5 * jnp.tanh(0.5 * x) + 0.5


def _silu(x):
    hx = 0.5 * x
    return hx * jnp.tanh(hx) + hx


def _softplus(x):
    return jnp.maximum(x, 0.0) + jnp.log1p(jnp.exp(-jnp.abs(x)))


def _gelu_tanh(x):
    c = np.float32(np.sqrt(2.0 / np.pi))
    c3 = np.float32(np.sqrt(2.0 / np.pi) * 0.044715)
    hx = 0.5 * x
    return hx + hx * jnp.tanh(x * (c + c3 * (x * x)))


def _row_iota(shape):
    return lax.broadcasted_iota(jnp.int32, shape, 0)


def _col_iota(shape):
    return lax.broadcasted_iota(jnp.int32, shape, 1)


def _chunk_cumsum(x, chunk):
    pos = _row_iota(x.shape) % chunk
    d = 1
    while d < chunk:
        x = x + jnp.where(pos >= d, pltpu.roll(x, d, 0), 0.0)
        d *= 2
    return x


def _causal_conv(hist_ref, x_ref, col0, width, w_ref, first_tile):
    assert w_ref.shape[0] == CONV_K == 4
    ts = x_ref.shape[0]
    strips = []
    for c0 in range(0, width, V7X_LANES):
        cs = slice(c0, c0 + V7X_LANES)
        x = x_ref[:, col0 + c0:col0 + c0 + V7X_LANES]
        hist = jnp.where(first_tile, 0.0, hist_ref[:, cs])
        ext = jnp.concatenate([hist, x], axis=0)
        prev = pltpu.roll(ext, 1, 0)
        near = ext * w_ref[3:4, cs] + prev * w_ref[2:3, cs]
        far = ext * w_ref[1:2, cs] + prev * w_ref[0:1, cs]
        acc = near[CONV_PAD:CONV_PAD + ts, :] + pltpu.roll(far, 2, 0)[CONV_PAD:CONV_PAD + ts, :]
        hist_ref[:, cs] = x[ts - CONV_PAD:ts, :]
        strips.append(acc)
    return strips


def _lookahead_tile(nt):
    s = pl.program_id(0)
    tile = jnp.maximum(s - 1, 0)
    return lax.rem(tile, nt) == 0


def _zero_on_first_step(*refs):
    @pl.when(pl.program_id(0) == 0)
    def _():
        for ref in refs:
            ref[...] = jnp.zeros(ref.shape, ref.dtype)


class _Lookahead:
    def __init__(self, xb_ref, w_ref, pn_ref, cur_ref, piece_cols):
        cur_ref[...] = pn_ref[...]
        self._refs = (xb_ref, w_ref, pn_ref)
        width = w_ref.shape[1]
        self._todo = [(c0, min(c0 + piece_cols, width)) for c0 in range(0, width, piece_cols)]

    def issue(self, n=1):
        xb_ref, w_ref, pn_ref = self._refs
        for _ in range(min(n, len(self._todo))):
            c0, c1 = self._todo.pop(0)
            pn_ref[:, c0:c1] = _dot(xb_ref[...], w_ref[:, c0:c1])

    def finish(self):
        self.issue(len(self._todo))


def _mixer_call(body, xb, w, params, *, layer, n_tiles, ts, out_w, scratch, name, lag=1):
    D = xb.shape[1]
    T = xb.shape[0]
    return pl.pallas_call(
        body,
        out_shape=jax.ShapeDtypeStruct((T, out_w), MXU_DTYPE),
        grid=(n_tiles + lag,),
        in_specs=[
            pl.BlockSpec((ts, D), lambda s: (jnp.minimum(s, n_tiles - 1), 0)),
            _layer_spec(w, layer),
        ] + [_layer_spec(p, layer) for p in params],
        out_specs=pl.BlockSpec((ts, out_w), lambda s: (jnp.maximum(s - lag, 0), 0)),
        scratch_shapes=[pltpu.VMEM((ts, w.shape[-1]), F32),
                        pltpu.VMEM((ts, w.shape[-1]), F32)
                        ] + scratch,
        compiler_params=pltpu.CompilerParams(
            dimension_semantics=("arbitrary",),
            vmem_limit_bytes=V7X_VMEM_LIMIT_BYTES),
        name=name,
    )(xb, w, *params)


def _layer_spec(a, layer):
    return pl.BlockSpec((None,) + a.shape[1:], lambda *_: (layer,) + (0,) * (a.ndim - 1))


def _rglru_kernel(nt, xb_ref, w_ref, cw_ref, cb_ref, wr_ref, br_ref, wi_ref, bi_ref, lam_ref,
                  o_ref, pn_ref, cur_ref, hist_ref, a_ref, u_ref, g_ref, carry_ref):
    ts = xb_ref.shape[0]
    W = cw_ref.shape[1]
    s = pl.program_id(0)
    first_scan = lax.rem(jnp.maximum(s - 2, 0), nt) == 0
    first = _lookahead_tile(nt)
    _zero_on_first_step(pn_ref, hist_ref, carry_ref, a_ref, u_ref, g_ref)
    rows = _row_iota((V7X_SUBLANES, W))

    def scan8(r0, h_prev):
        ca = a_ref[r0:r0 + V7X_SUBLANES, :]
        cb = u_ref[r0:r0 + V7X_SUBLANES, :]
        d = 1
        while d < V7X_SUBLANES:
            keep = rows >= d
            cb = jnp.where(keep, ca * pltpu.roll(cb, d, 0) + cb, cb)
            ca = jnp.where(keep, ca * pltpu.roll(ca, d, 0), ca)
            d *= 2
        h = ca * h_prev + cb
        return h, jnp.broadcast_to(h[V7X_SUBLANES - 1:V7X_SUBLANES, :], (V7X_SUBLANES, W))

    h_prev = jnp.where(first_scan, 0.0, carry_ref[...])
    for r0 in range(0, ts, OUT_ROWS):
        hs = []
        for j in range(OUT_ROWS // V7X_SUBLANES):
            h, h_prev = scan8(r0 + j * V7X_SUBLANES, h_prev)
            hs.append(h)
        gate = _gelu_tanh(g_ref[r0:r0 + OUT_ROWS, :])
        o_ref[r0:r0 + OUT_ROWS, :] = (jnp.concatenate(hs, axis=0) * gate).astype(o_ref.dtype)
    carry_ref[...] = h_prev

    look = _Lookahead(xb_ref, w_ref, pn_ref, cur_ref, w_ref.shape[1])
    look.finish()
    g_ref[...] = cur_ref[:, W:2 * W]
    xa = jnp.concatenate(_causal_conv(hist_ref, cur_ref, 0, W, cw_ref, first), axis=-1) + cb_ref[...]
    xab = _mx(xa)
    r = _sigmoid(_dot(xab, wr_ref[...]) + br_ref[...])
    i = _sigmoid(_dot(xab, wi_ref[...]) + bi_ref[...])
    log_a = (-A_C) * r * _softplus(-lam_ref[...])
    a = jnp.exp(log_a)
    a_ref[...] = a
    one_m_a2 = 1.0 - a * a
    u_ref[...] = (one_m_a2 * lax.rsqrt(jnp.maximum(one_m_a2, SQRT_GUARD))) * (i * xa)


def _rglru(xb, w, cw, cb, wr, br, wi, bi, lam, *, layer, n_tiles, nt, ts):
    W = cw.shape[-1]
    params = (cw, cb, wr, br, wi, bi, lam)
    scratch = [
        pltpu.VMEM((CONV_PAD, W), F32),
        pltpu.VMEM((ts, W), F32),
        pltpu.VMEM((ts, W), F32),
        pltpu.VMEM((ts, W), F32),
        pltpu.VMEM((V7X_SUBLANES, W), F32),
    ]
    return _mixer_call(functools.partial(_rglru_kernel, nt), xb, w, params, layer=layer, lag=2,
                       n_tiles=n_tiles, ts=ts, out_w=W, scratch=scratch, name="rglru")


def _gdn_kernel(nt, xb_ref, w_ref, cw_ref, alog_ref, dtb_ref, nw_ref,
                o_ref, pn_ref, cur_ref, hist_ref, state_ref, lmask_ref):
    ts = xb_ref.shape[0]
    C = GDN_CHUNK
    n_chunks = ts // C
    dk = V7X_LANES
    hw = B_HEADS * dk
    wq = cw_ref.shape[1]
    first = _lookahead_tile(nt)
    _zero_on_first_step(pn_ref, hist_ref, state_ref)

    @pl.when(pl.program_id(0) == 0)
    def _():
        ri0 = _row_iota((C, C))
        ci0 = _col_iota((C, C))
        for lv in range(lmask_ref.shape[0]):
            n = INV_BASE << lv
            lm = jnp.logical_and((ri0 // (2 * n)) == (ci0 // (2 * n)), (ri0 // n) != (ci0 // n))
            lmask_ref[lv] = lm.astype(F32).astype(MXU_DTYPE)

    look = _Lookahead(xb_ref, w_ref, pn_ref, cur_ref, w_ref.shape[1])
    look.finish()
    sm = cur_ref[:, wq + hw:wq + hw + SMALL_W]
    qkv = [_silu(s) for s in _causal_conv(hist_ref, cur_ref, 0, wq, cw_ref, first)]
    beta_all = _sigmoid(sm)
    g_all = -jnp.exp(alog_ref[...]) * _softplus(sm + dtb_ref[...])
    gcum_all = _chunk_cumsum(g_all, C)
    gcum_t = gcum_all.T

    ri = _row_iota((C, C))
    ci = _col_iota((C, C))
    lower = ci <= ri
    strict = ci < ri
    eye = (ci == ri).astype(F32)
    scale = np.float32(dk ** -0.5)
    base_mask = (ri // INV_BASE) == (ci // INV_BASE)
    level_masks = [lmask_ref[lv] for lv in range(lmask_ref.shape[0])]

    chains = [(c, h) for c in range(n_chunks) for h in range(B_HEADS)]
    st = {}
    for c, h in chains:
        rs = slice(c * C, (c + 1) * C)
        q = qkv[h][rs]
        k = qkv[B_HEADS + h][rs]
        v = qkv[2 * B_HEADS + h][rs]
        q = q * (lax.rsqrt(jnp.sum(q * q, -1, keepdims=True) + L2_EPS) * scale)
        k = k * lax.rsqrt(jnp.sum(k * k, -1, keepdims=True) + L2_EPS)
        beta = beta_all[rs, h:h + 1]
        gcol = gcum_all[rs, B_HEADS + h:B_HEADS + h + 1]
        grow = gcum_t[B_HEADS + h:B_HEADS + h + 1, rs]
        decay = jnp.where(lower, jnp.exp(jnp.where(lower, gcol - grow, 0.0)), 0.0)
        qb, kb = _mx(q), _mx(k)
        kk = _dot_nt(kb, kb)
        attn_b = _mx(_dot_nt(qb, kb) * decay)
        a_mat = jnp.where(strict, beta * kk * decay, 0.0)
        a_base = jnp.where(base_mask, a_mat, 0.0)
        st[c, h] = dict(q=q, k=k, v=v, beta=beta, gcol=gcol, attn_b=attn_b, a_b=_mx(a_mat),
                        p=eye - a_base, x=a_base)
    span = 2
    while span < INV_BASE:
        for key in chains:
            d = st[key]
            xb = _mx(d["x"])
            d["x"] = _dot(xb, xb)
        for key in chains:
            d = st[key]
            d["p"] = d["p"] + _dot(_mx(d["p"]), _mx(d["x"]))
        span *= 2
    for lm in level_masks:
        for key in chains:
            d = st[key]
            d["pb"] = _mx(d["p"])
            d["m"] = _dot(d["a_b"] * lm, d["pb"])
        for key in chains:
            d = st[key]
            d["p"] = d["p"] - _dot(d["pb"], _mx(d["m"]))
    for key in chains:
        d = st[key]
        e_g = jnp.exp(d["gcol"])
        rhs = jnp.concatenate([d["v"] * d["beta"], d["k"] * (d["beta"] * e_g)], axis=-1)
        d["solb"] = _mx(_dot(_mx(d["p"]), _mx(rhs)))
        g_last = d["gcol"][C - 1:C, :]
        d.update(q_g=d["q"] * e_g, kdb=_mx(d["k"] * jnp.exp(g_last - d["gcol"])),
                 dl=jnp.exp(g_last))
    for key in chains:
        d = st[key]
        kd_sol = _dot_tn(d["kdb"], d["solb"])
        at_sol = _dot(d["attn_b"], d["solb"])
        d.update(o0=at_sol[:, :dk], n_mat=kd_sol[:, :dk],
                 lhs=_mx(jnp.concatenate([d["q_g"] - at_sol[:, dk:], kd_sol[:, dk:]], axis=0)))
    states = [jnp.where(first, 0.0, state_ref[h]) for h in range(B_HEADS)]
    for c in range(n_chunks):
        rs = slice(c * C, (c + 1) * C)
        for h in range(B_HEADS):
            d = st[c, h]
            prod = _dot(d["lhs"], _mx(states[h]))
            o = prod[:C] + d["o0"]
            states[h] = states[h] * d["dl"] - prod[C:] + d["n_mat"]
            z = cur_ref[rs, wq + h * dk:wq + (h + 1) * dk]
            y = (o * lax.rsqrt(jnp.mean(o * o, -1, keepdims=True) + RMS_EPS) * nw_ref[...]
                 * _silu(z))
            o_ref[rs, h * dk:(h + 1) * dk] = y.astype(o_ref.dtype)
    for h in range(B_HEADS):
        state_ref[h] = states[h]


def _gdn(xb, w, cw, alog, dtb, nw, *, layer, n_tiles, nt, ts):
    wq = cw.shape[-1]
    wz = B_HEADS * V7X_LANES
    params = (cw, alog, dtb, nw)
    scratch = [
        pltpu.VMEM((CONV_PAD, wq), F32),
        pltpu.VMEM((B_HEADS, V7X_LANES, V7X_LANES), F32),
        pltpu.VMEM((GDN_INV_LEVELS, GDN_CHUNK, GDN_CHUNK), MXU_DTYPE),
    ]
    return _mixer_call(functools.partial(_gdn_kernel, nt), xb, w, params, layer=layer,
                       n_tiles=n_tiles, ts=ts, out_w=wz, scratch=scratch, name="gdn")


def _gla_kernel(nt, xb_ref, w_ref, wg_ref, bg_ref, nw_ref, o_ref, pn_ref, cur_ref, state_ref):
    ts = xb_ref.shape[0]
    C = GLA_CHUNK
    n_chunks = ts // C
    AT = GLA_ATTN_TILE
    kw = wg_ref.shape[1]
    dv = V7X_LANES
    wv = C_HEADS * dv
    pair_k = V7X_LANES
    dk = pair_k // 2
    n_pairs = kw // pair_k
    first = _lookahead_tile(nt)
    _zero_on_first_step(pn_ref, state_ref)
    look = _Lookahead(xb_ref, w_ref, pn_ref, cur_ref, PROJ_PIECE)
    sm = cur_ref[:, 2 * kw + 2 * wv:2 * kw + 2 * wv + SMALL_W]
    q = cur_ref[:, 0:kw] * np.float32(dk ** -0.5)
    k = cur_ref[:, kw:2 * kw]
    vb = _mx(cur_ref[:, 2 * kw:2 * kw + wv])

    zg = _dot(_mx(sm), wg_ref[...]) + bg_ref[...]
    look.issue(GLA_PIECES_UP_FRONT)
    log_a = -_softplus(-zg) * np.float32(1.0 / C_TAU)
    b_cum = _chunk_cumsum(log_a, C)
    q_in = q * jnp.exp(b_cum)
    kinb = _mx(k * jnp.exp(-b_cum))

    ri = _row_iota((AT, AT))
    ci = _col_iota((AT, AT))
    causal = jnp.logical_and(ci <= ri, (ri // C) == (ci // C))
    lane_head = _col_iota((AT, pair_k)) // dk
    bd_mask = (_row_iota((pair_k, 2 * dv)) // dk) == (_col_iota((pair_k, 2 * dv)) // dv)

    o_intra = {}
    for a in range(ts // AT):
        ra = slice(a * AT, (a + 1) * AT)
        for p in range(n_pairs):
            ks = slice(p * pair_k, (p + 1) * pair_k)
            for e in range(2):
                qm = _mx(jnp.where(lane_head == e, q_in[ra, ks], 0.0))
                attn = jnp.where(causal, _dot_nt(qm, kinb[ra, ks]), 0.0)
                hcol = slice((2 * p + e) * dv, (2 * p + e + 1) * dv)
                o_intra[a, 2 * p + e] = _dot(_mx(attn), vb[ra, hcol])
    o_inter = {}
    for p in range(n_pairs):
        ks = slice(p * pair_k, (p + 1) * pair_k)
        b_p = b_cum[:, ks]
        b_last_rows = [b_p[(c + 1) * C - 1:(c + 1) * C, :] for c in range(n_chunks)]
        b_last_blk = jnp.concatenate(
            b_last_rows + [jnp.zeros((V7X_LANES - n_chunks, pair_k), F32)], axis=0)
        d_last_cols = jnp.exp(b_last_blk).T
        upd = []
        for c in range(n_chunks):
            rs = slice(c * C, (c + 1) * C)
            k_dec = k[rs, ks] * jnp.exp(b_last_rows[c] - b_p[rs])
            upd.append(jnp.where(bd_mask, _dot_tn(_mx(k_dec), vb[rs, p * 2 * dv:(p + 1) * 2 * dv]), 0.0))
        state = jnp.where(first, 0.0, state_ref[p])
        qb_p = _mx(q_in[:, ks])
        for c in range(n_chunks):
            rs = slice(c * C, (c + 1) * C)
            o_inter[p, c] = _dot(qb_p[rs], _mx(state))
            state = state * d_last_cols[:, c:c + 1] + upd[c]
        state_ref[p] = state
    look.finish()
    for p in range(n_pairs):
        o_int = jnp.concatenate([o_inter[p, c] for c in range(n_chunks)], axis=0)
        for e in range(2):
            hd = 2 * p + e
            hcol = slice(hd * dv, (hd + 1) * dv)
            o = (jnp.concatenate([o_intra[a, hd] for a in range(ts // AT)], axis=0)
                 + o_int[:, e * dv:(e + 1) * dv])
            r_gate = _silu(cur_ref[:, 2 * kw + wv + hd * dv:2 * kw + wv + (hd + 1) * dv])
            y = (o * lax.rsqrt(jnp.mean(o * o, -1, keepdims=True) + RMS_EPS) * nw_ref[...]
                 * r_gate)
            o_ref[:, hcol] = y.astype(o_ref.dtype)


def _gla(xb, w, wg, bg, nw, *, layer, n_tiles, nt, ts):
    kw = wg.shape[-1]
    wv = C_HEADS * V7X_LANES
    params = (wg, bg, nw)
    scratch = [
        pltpu.VMEM((kw // V7X_LANES, V7X_LANES, 2 * V7X_LANES), F32),
    ]
    return _mixer_call(functools.partial(_gla_kernel, nt), xb, w, params, layer=layer,
                       n_tiles=n_tiles, ts=ts, out_w=wv, scratch=scratch, name="gla")


def _layer_norm(h, g, b):
    mu = jnp.mean(h, -1, keepdims=True)
    hc = h - mu
    var = jnp.mean(hc * hc, -1, keepdims=True)
    return hc * lax.rsqrt(var + LN_EPS) * g + b


def _merge_kernel(alpha, sub, ya_ref, yb_ref, yc_ref, xb_ref, x_ref, wm_ref, wb_ref, gb_ref,
                  wo_ref, g_ref, b_ref, o_ref, ob_ref):
    tm, D = x_ref.shape
    for r0 in range(0, tm, sub):
        rs = slice(r0, r0 + sub)
        xb = xb_ref[rs, :]
        merged = None
        for gi, y_ref in enumerate((ya_ref, yb_ref, yc_ref)):
            cs = slice(gi * D, (gi + 1) * D)
            gate = _sigmoid(_dot(xb, wm_ref[:, cs]) + gb_ref[:, cs])
            term = gate * _dot(y_ref[rs, :], wb_ref[gi])
            merged = term if merged is None else merged + term
        mix = _dot(_mx(merged), wo_ref[...])
        y = _layer_norm(alpha * x_ref[rs, :] + mix, g_ref[...], b_ref[...])
        o_ref[rs, :] = y
        ob_ref[rs, :] = _mx(y)


def _merge_out(ya, yb, yc, xb, x, wm, wb, gb, wo, ln_g, ln_b, *, layer, alpha, tm, sub):
    T, D = x.shape
    W = ya.shape[1]
    row = lambda w: pl.BlockSpec((tm, w), lambda i: (i, 0))
    params = (wm, wb, gb, wo, ln_g, ln_b)
    return pl.pallas_call(
        functools.partial(_merge_kernel, alpha, sub),
        out_shape=(jax.ShapeDtypeStruct((T, D), F32), jax.ShapeDtypeStruct((T, D), MXU_DTYPE)),
        grid=(T // tm,),
        in_specs=[row(W), row(W), row(W), row(D), row(D)] + [_layer_spec(p, layer) for p in params],
        out_specs=(row(D), row(D)),
        compiler_params=pltpu.CompilerParams(
            dimension_semantics=("parallel",),
            vmem_limit_bytes=V7X_VMEM_LIMIT_BYTES),
        name="merge_out",
    )(ya, yb, yc, xb, x, wm, wb, gb, wo, ln_g, ln_b)


def _ffn_kernel(alpha, sub, x_ref, xb_ref, w1_ref, w3_ref, w2_ref, g_ref, b_ref, o_ref, ob_ref):
    tm = x_ref.shape[0]
    for r0 in range(0, tm, sub):
        rs = slice(r0, r0 + sub)
        xb = xb_ref[rs, :]
        h = _silu(_dot(xb, w1_ref[...])) * _dot(xb, w3_ref[...])
        y = _layer_norm(alpha * x_ref[rs, :] + _dot(_mx(h), w2_ref[...]), g_ref[...], b_ref[...])
        o_ref[rs, :] = y
        ob_ref[rs, :] = _mx(y)


def _ffn(x, xb, w1, w3, w2, ln_g, ln_b, *, layer, alpha, tm, sub):
    T, D = x.shape
    row = pl.BlockSpec((tm, D), lambda i: (i, 0))
    return pl.pallas_call(
        functools.partial(_ffn_kernel, alpha, sub),
        out_shape=(jax.ShapeDtypeStruct((T, D), F32), jax.ShapeDtypeStruct((T, D), MXU_DTYPE)),
        grid=(T // tm,),
        in_specs=[row, row] + [_layer_spec(p, layer) for p in (w1, w3, w2, ln_g, ln_b)],
        out_specs=(row, row),
        compiler_params=pltpu.CompilerParams(
            dimension_semantics=("parallel",),
            vmem_limit_bytes=V7X_VMEM_LIMIT_BYTES),
        name="ffn",
    )(x, xb, w1, w3, w2, ln_g, ln_b)


def _tile(n, pref):
    t = min(n, pref)
    while n % t:
        t -= 1
    return t


def _pad_lanes(v, offset, width):
    return jnp.zeros((v.shape[0], 1, width), F32).at[:, 0, offset:offset + v.shape[1]].set(v)


def kernel(x, w_in, a_conv_w, a_conv_b, a_w_r, a_b_r, a_w_i, a_b_i, a_lambda, b_conv_w, b_a_log,
           b_dt_bias, b_norm_w, c_w_g2, c_b_g2, c_norm_w, gate_b, w_branch, w_out, ln1_g, ln1_b,
           ffn_w1, ffn_w3, ffn_w2, ln2_g, ln2_b):
    B, S, D = x.shape
    depth = w_in.shape[0]
    T = B * S
    a_width = a_conv_w.shape[2]
    qkv_w = b_conv_w.shape[2]
    b_width = B_HEADS * V7X_LANES
    c_kw = c_w_g2.shape[2]
    c_rank = c_w_g2.shape[1]
    c_width = C_HEADS * V7X_LANES
    alpha = float((2.0 * depth) ** 0.25)

    splits = (a_width, a_width, qkv_w, b_width, B_HEADS, B_HEADS, c_kw, c_kw, c_width, c_rank,
              c_width, N_BRANCH * D)
    offs = np.concatenate([[0], np.cumsum(splits)])
    (s_ax, s_ag, s_qkv, s_z, s_beta, s_alpha, s_cq, s_ck, s_cv, s_cg, s_cr, s_merge) = [
        slice(int(offs[i]), int(offs[i + 1])) for i in range(len(splits))]
    n_small = 2 * B_HEADS + c_rank
    w_in_b = _mx(w_in)
    w_small = jnp.concatenate(
        [w_in_b[:, :, s_beta], w_in_b[:, :, s_alpha], w_in_b[:, :, s_cg],
         jnp.zeros((depth, D, SMALL_W - n_small), MXU_DTYPE)], axis=-1)

    def cols(*groups):
        return jnp.concatenate([g if not isinstance(g, slice) else w_in_b[:, :, g] for g in groups],
                               axis=-1)

    w_a = cols(s_ax, s_ag)
    w_b = cols(s_qkv, s_z, w_small)
    w_c = cols(s_cq, s_ck, s_cv, s_cr, w_small)
    w_m = cols(s_merge)

    eye_blocks = jnp.eye(A_BLOCKS, dtype=F32)

    def block_diag(w):
        return jnp.einsum("lhij,hg->lhigj", w, eye_blocks).reshape(depth, a_width, a_width)

    wr_bd = block_diag(a_w_r).astype(MXU_DTYPE)
    wi_bd = block_diag(a_w_i).astype(MXU_DTYPE)
    wg_pad = jnp.zeros((depth, SMALL_W, c_kw), F32).at[:, 2 * B_HEADS:n_small, :].set(c_w_g2)
    wg_pad = wg_pad.astype(MXU_DTYPE)
    wb = w_branch.astype(MXU_DTYPE)
    wo = w_out.astype(MXU_DTYPE)
    w1 = ffn_w1.astype(MXU_DTYPE)
    w3 = ffn_w3.astype(MXU_DTYPE)
    w2 = ffn_w2.astype(MXU_DTYPE)

    def mixer_tiles(pref):
        ts = _tile(S, pref)
        return dict(ts=ts, nt=S // ts, n_tiles=B * (S // ts))

    mix = mixer_tiles(512)
    mix_gla = mixer_tiles(1024)
    assert mix["ts"] % GDN_CHUNK == 0 and mix["ts"] % OUT_ROWS == 0
    assert mix_gla["ts"] % GLA_ATTN_TILE == 0 and GLA_ATTN_TILE % GLA_CHUNK == 0
    tm_merge = _tile(T, 1024)
    sub_merge = _tile(tm_merge, 256)
    tm_ffn = _tile(T, 1024)
    sub_ffn = _tile(tm_ffn, 256)

    row = lambda v: v[:, None, :]
    alog_pad = _pad_lanes(b_a_log, B_HEADS, SMALL_W)
    dtb_pad = _pad_lanes(b_dt_bias, B_HEADS, SMALL_W)

    h = x.reshape(T, D)
    hb = _mx(h)
    for l in range(depth):
        ya = _rglru(hb, w_a, a_conv_w, row(a_conv_b), wr_bd, row(a_b_r), wi_bd, row(a_b_i),
                    row(a_lambda), layer=l, **mix)
        yb = _gdn(hb, w_b, b_conv_w, alog_pad, dtb_pad, row(b_norm_w), layer=l, **mix)
        yc = _gla(hb, w_c, wg_pad, row(c_b_g2), row(c_norm_w), layer=l, **mix_gla)
        h, hb = _merge_out(ya, yb, yc, hb, h, w_m, wb, row(gate_b), wo, row(ln1_g), row(ln1_b),
                           layer=l, alpha=alpha, tm=tm_merge, sub=sub_merge)
        h, hb = _ffn(h, hb, w1, w3, w2, row(ln2_g), row(ln2_b),
                     layer=l, alpha=alpha, tm=tm_ffn, sub=sub_ffn)
    return h.reshape(B, S, D)
```
